```python
import jax, jax.numpy as jnp
from jax import lax
import numpy as np

D_MODEL = 1024
BATCH = 16
SEQ = 2048
DEPTH = 4

CTX_LEN = 256
GRID_W = 64
N_MOD = 6

RW_HEADS = 8
RW_HEAD_DIM = 64
RW_DIM = RW_HEADS * RW_HEAD_DIM
RW_DECAY_LORA = 64
RW_ICLR_LORA = 64
RW_GATE_LORA = 128
RW_GN_EPS = 64e-5
RW_IN = 3 * RW_DIM + 2 * RW_DECAY_LORA + 2 * RW_ICLR_LORA + RW_GATE_LORA

HG_HEADS = 4
HG_KEY_DIM = 128
HG_VAL_DIM = 128
HG_DIM = HG_HEADS * HG_KEY_DIM
HG_VDIM = HG_HEADS * HG_VAL_DIM
HG_CHUNK = 64
HG_IN = 3 * HG_DIM + 2 * HG_VDIM
HG_F_FLOOR = 1e-30

EVEN_IN = RW_IN + HG_IN
EVEN_OUT = RW_DIM + HG_VDIM

MLA_HEADS = 16
MLA_NOPE = 64
MLA_ROPE = 32
MLA_V = 64
MLA_Q_RANK = 256
MLA_KV_RANK = 128
MLA_IN = MLA_Q_RANK + MLA_KV_RANK + MLA_ROPE
MLA_QK = MLA_NOPE + MLA_ROPE
MLA_SCALE = MLA_QK ** -0.5
ATTN_BLOCK = 128
ROPE_BASE = 10000.0

N_EXPERTS = 32
TOP_K = 4
EXPERT_DIM = 1024
SWIGLU_LIMIT = 7.0
SWIGLU_ALPHA = 1.702

N_EVEN = (DEPTH + 1) // 2
N_ODD = DEPTH // 2
DEEPNORM_ALPHA = (2 * DEPTH) ** 0.25
DEEPNORM_BETA = (8 * DEPTH) ** -0.25
LN_EPS = 1e-5
RMS_EPS = 1e-6

kernel_name = 'hybrid_rwkv7_hgrn2_mla_moe_dit'


def split_sizes(z, sizes):
    idx, acc = [], 0
    for s in sizes[:-1]:
        acc += s
        idx.append(acc)
    return jnp.split(z, idx, axis=-1)


def layer_norm(x, g, b):
    xf = x.astype(jnp.float32)
    mu = jnp.mean(xf, -1, keepdims=True)
    var = jnp.mean(jnp.square(xf - mu), -1, keepdims=True)
    return ((xf - mu) * lax.rsqrt(var + LN_EPS)).astype(x.dtype) * g + b


def rms_normalize(x):
    xf = x.astype(jnp.float32)
    return xf * lax.rsqrt(jnp.mean(jnp.square(xf), -1, keepdims=True) + RMS_EPS)


def rms_norm(x, g):
    return rms_normalize(x).astype(x.dtype) * g


def modulate(h, shift, scale):
    return h * (1 + scale) + shift


def centred_shift(z):
    prev = jnp.pad(z[:, :-1], ((0, 0), (1, 0), (0, 0)))
    nxt = jnp.pad(z[:, 1:], ((0, 0), (0, 1), (0, 0)))
    return 0.5 * (prev + nxt) - z


def reverse_segments(z, n_ctx):
    return jnp.concatenate([jnp.flip(z[:, :n_ctx], 1), jnp.flip(z[:, n_ctx:], 1)], axis=1)


def scan_order(z_fwd, z_bwd, n_ctx):
    return jnp.stack([z_fwd, reverse_segments(z_bwd, n_ctx)])


def merge_dirs(y, n_ctx):
    return y[0] + reverse_segments(y[1], n_ctx)


def rwkv7_scan(r, w, k, v, aa, bb):
    d2, b, _, h, n = r.shape
    xs = tuple(jnp.moveaxis(z.astype(jnp.float32), 2, 0) for z in (r, w, k, v, aa, bb))

    def step(state, inp):
        r_t, w_t, k_t, v_t, a_t, b_t = inp
        sa = jnp.einsum('dbhvk,dbhk->dbhv', state, a_t)
        state = (state * w_t[..., None, :] + sa[..., :, None] * b_t[..., None, :]
                 + v_t[..., :, None] * k_t[..., None, :])
        return state, jnp.einsum('dbhvk,dbhk->dbhv', state, r_t)

    _, y = lax.scan(step, jnp.zeros((d2, b, h, n, n), jnp.float32), xs)
    return jnp.moveaxis(y, 0, 2)


def rwkv7_group(za, n_ctx, w0, w2, a0, a2, g2, k_k, k_a, r_k, gn_g, gn_b):
    b, t, _ = za.shape
    to_heads = lambda z: z.reshape(*z.shape[:-1], RW_HEADS, RW_HEAD_DIM)
    r, k, v, wd, ad, gd = split_sizes(
        za, [RW_DIM, RW_DIM, RW_DIM, 2 * RW_DECAY_LORA, 2 * RW_ICLR_LORA, RW_GATE_LORA])
    wd = wd.reshape(b, t, 2, RW_DECAY_LORA)
    ad = ad.reshape(b, t, 2, RW_ICLR_LORA)
    w_pre = (w0 + jnp.einsum('btdr,drc->btdc', jnp.tanh(wd), w2)).astype(jnp.float32)
    decay = jnp.exp(-jnp.exp(-jax.nn.softplus(-w_pre) - 0.5))
    a = jax.nn.sigmoid(a0 + jnp.einsum('btdr,drc->btdc', ad, a2))
    g = jax.nn.sigmoid(gd) @ g2
    kk = to_heads((k * k_k).astype(jnp.float32))
    kk = (kk * lax.rsqrt(jnp.maximum(jnp.sum(jnp.square(kk), -1, keepdims=True), 1e-24)))
    kk = kk.reshape(b, t, RW_DIM)
    k_dir = k[:, :, None] * (1 + (a - 1) * k_a)
    bb = kk[:, :, None] * a
    shared = lambda z: to_heads(scan_order(z, z, n_ctx))
    per_dir = lambda z: to_heads(scan_order(z[:, :, 0], z[:, :, 1], n_ctx))
    y = rwkv7_scan(shared(r), per_dir(decay), per_dir(k_dir), shared(v), shared(-kk), per_dir(bb))
    y = merge_dirs(y, n_ctx)
    mu = jnp.mean(y, -1, keepdims=True)
    var = jnp.mean(jnp.square(y - mu), -1, keepdims=True)
    y = ((y - mu) * lax.rsqrt(var + RW_GN_EPS)).reshape(b, t, RW_DIM).astype(za.dtype)
    y = y * gn_g + gn_b
    bonus = jnp.einsum('bthn,btdhn,hn->bth', to_heads(r), to_heads(k_dir), r_k)[..., None] * to_heads(v)
    return (y + bonus.reshape(b, t, RW_DIM)) * g


def hgrn2_chunk_scan(q, k, log_f, v):
    d2, b, t, h, _ = q.shape
    nc = t // HG_CHUNK

    def chunks(z):
        z = z.astype(jnp.float32).reshape(d2, b, nc, HG_CHUNK, h, z.shape[-1])
        return jnp.transpose(z, (2, 0, 1, 4, 3, 5))

    lower_tri = jnp.tril(jnp.ones((HG_CHUNK, HG_CHUNK), bool))[:, :, None]

    def step(state, inp):
        q_c, k_c, g_c, v_c = inp
        cum = jnp.cumsum(g_c, axis=-2)
        diff = cum[..., :, None, :] - cum[..., None, :, :]
        decay = jnp.where(lower_tri, jnp.exp(jnp.minimum(diff, 0.0)), 0.0)
        scores = jnp.einsum('dbhtk,dbhsk,dbhtsk->dbhts', q_c, k_c, decay)
        out = (jnp.einsum('dbhts,dbhsv->dbhtv', scores, v_c)
               + jnp.einsum('dbhtk,dbhkv->dbhtv', q_c * jnp.exp(cum), state))
        last = cum[..., -1:, :]
        state = (state * jnp.exp(last)[..., 0, :, None]
                 + jnp.einsum('dbhsk,dbhsv->dbhkv', k_c * jnp.exp(last - cum), v_c))
        return state, out

    s0 = jnp.zeros((d2, b, h, q.shape[-1], v.shape[-1]), jnp.float32)
    _, o = lax.scan(step, s0, tuple(chunks(z) for z in (q, k, log_f, v)))
    return jnp.transpose(o, (1, 2, 0, 4, 3, 5)).reshape(d2, b, t, h, v.shape[-1])


def hgrn2_group(zb, n_ctx, lower, norm_g):
    b, t, _ = zb.shape
    kh = lambda z: z.reshape(*z.shape[:-1], HG_HEADS, HG_KEY_DIM)
    vh = lambda z: z.reshape(*z.shape[:-1], HG_HEADS, HG_VAL_DIM)
    q, f_logit, i, g = split_sizes(zb, [HG_DIM, 2 * HG_DIM, HG_VDIM, HG_VDIM])
    f_logit = f_logit.reshape(b, t, 2, HG_DIM).astype(jnp.float32)
    sig = jax.nn.sigmoid(f_logit)
    f = lower + (1 - lower) * sig
    log_f = jnp.log(jnp.maximum(f, HG_F_FLOOR))
    k_in = (1 - lower) * (1 - sig)
    q = jax.nn.silu(q)
    o = hgrn2_chunk_scan(kh(scan_order(q, q, n_ctx)),
                         kh(scan_order(k_in[:, :, 0], k_in[:, :, 1], n_ctx)),
                         kh(scan_order(log_f[:, :, 0], log_f[:, :, 1], n_ctx)),
                         vh(scan_order(i, i, n_ctx)))
    o = rms_normalize(merge_dirs(o, n_ctx)).reshape(b, t, HG_VDIM).astype(zb.dtype) * norm_g
    return o * jax.nn.silu(g)


def rwkv7_hgrn2_mixer(u_ctx, u_lat, w_in, mu, w0, w2, a0, a2, g2, k_k, k_a, r_k, gn_g, gn_b,
                      lower, hg_norm_g, w_out, need_ctx):
    n_ctx = u_ctx.shape[1]

    def project(u):
        z = u @ w_in
        za, zb = z[..., :RW_IN], z[..., RW_IN:]
        return za + mu * centred_shift(za), zb

    za_c, zb_c = project(u_ctx)
    za_l, zb_l = project(u_lat)
    za = jnp.concatenate([za_c, za_l], axis=1)
    zb = jnp.concatenate([zb_c, zb_l], axis=1)
    y = jnp.concatenate([rwkv7_group(za, n_ctx, w0, w2, a0, a2, g2, k_k, k_a, r_k, gn_g, gn_b),
                         hgrn2_group(zb, n_ctx, lower, hg_norm_g)], axis=-1)
    if need_ctx:
        y = y @ w_out
        return y[:, :n_ctx], y[:, n_ctx:]
    return None, y[:, n_ctx:] @ w_out


def axial_rope_tables(rows):
    t = jnp.arange(rows * GRID_W)
    row = (t // GRID_W).astype(jnp.float32)
    col = (t % GRID_W).astype(jnp.float32)
    half = MLA_ROPE // 2
    inv_freq = ROPE_BASE ** (-jnp.arange(0, half, 2, dtype=jnp.float32) / half)
    ang_r = row[:, None] * inv_freq
    ang_c = col[:, None] * inv_freq
    ang = jnp.concatenate([ang_r, ang_r, ang_c, ang_c], axis=-1)
    return jnp.cos(ang), jnp.sin(ang)


def rope(x, cos, sin):
    r1, r2, c1, c2 = jnp.split(x, 4, axis=-1)
    rot = jnp.concatenate([-r2, r1, -c2, c1], axis=-1)
    return x * cos.astype(x.dtype) + rot * sin.astype(x.dtype)


def mla_queries(zq, q_norm_g, w_qb, cos, sin):
    b, l, _ = zq.shape
    q = (rms_norm(zq, q_norm_g) @ w_qb).reshape(b, l, MLA_HEADS, MLA_QK)
    if cos is None:
        return q
    return jnp.concatenate([q[..., :MLA_NOPE], rope(q[..., MLA_NOPE:], cos[:, None], sin[:, None])], -1)


def mla_keys_values(zkv, kv_norm_g, w_kvb, cos, sin):
    b, l, _ = zkv.shape
    kva, kpe = zkv[..., :MLA_KV_RANK], zkv[..., MLA_KV_RANK:]
    kv = (rms_norm(kva, kv_norm_g) @ w_kvb).reshape(b, l, MLA_HEADS, MLA_NOPE + MLA_V)
    if cos is not None:
        kpe = rope(kpe, cos, sin)
    k = jnp.concatenate([kv[..., :MLA_NOPE],
                         jnp.broadcast_to(kpe[:, :, None], (b, l, MLA_HEADS, MLA_ROPE))], axis=-1)
    return k, kv[..., MLA_NOPE:]


def attend(q, k, v):
    s = jnp.einsum('bqhd,bkhd->bhqk', q, k).astype(jnp.float32) * MLA_SCALE
    p = jax.nn.softmax(s, axis=-1).astype(v.dtype)
    return jnp.einsum('bhqk,bkhd->bqhd', p, v)


def mla_mixer(u_ctx, u_lat, w_in, q_norm_g, w_qb, kv_norm_g, w_kvb, w_out, cos, sin, need_ctx):
    b, n_lat, _ = u_lat.shape
    z_lat = u_lat @ w_in
    q_lat = mla_queries(z_lat[..., :MLA_Q_RANK], q_norm_g, w_qb, cos, sin)
    k_lat, v_lat = mla_keys_values(z_lat[..., MLA_Q_RANK:], kv_norm_g, w_kvb, cos, sin)
    k_ctx, v_ctx = mla_keys_values(u_ctx @ w_in[:, MLA_Q_RANK:], kv_norm_g, w_kvb, None, None)
    k_all = jnp.concatenate([k_ctx, k_lat], axis=1)
    v_all = jnp.concatenate([v_ctx, v_lat], axis=1)
    n_blk = n_lat // ATTN_BLOCK
    q_blocks = jnp.moveaxis(q_lat.reshape(b, n_blk, ATTN_BLOCK, MLA_HEADS, MLA_QK), 1, 0)
    o = lax.map(lambda q_b: attend(q_b, k_all, v_all), q_blocks)
    y_lat = jnp.moveaxis(o, 0, 1).reshape(b, n_lat, MLA_HEADS * MLA_V) @ w_out
    if not need_ctx:
        return None, y_lat
    q_ctx = mla_queries(u_ctx @ w_in[:, :MLA_Q_RANK], q_norm_g, w_qb, None, None)
    y_ctx = attend(q_ctx, k_ctx, v_ctx).reshape(b, -1, MLA_HEADS * MLA_V) @ w_out
    return y_ctx, y_lat


def moe_clamped_swiglu(u, w_router, b_router, w_gu, b_gu, w_down, b_down):
    logits = (u @ w_router + b_router).astype(jnp.float32)
    top_logit, top_idx = lax.top_k(logits, TOP_K)
    gates = jnp.einsum('nk,nke->ne', jax.nn.softmax(top_logit, axis=-1),
                       jax.nn.one_hot(top_idx, N_EXPERTS, dtype=jnp.float32)).astype(u.dtype)
    out = jnp.zeros_like(u)
    for e in range(N_EXPERTS):
        glu, lin = jnp.split(u @ w_gu[e] + b_gu[e], 2, axis=-1)
        glu = jnp.minimum(glu, SWIGLU_LIMIT)
        lin = jnp.clip(lin, -SWIGLU_LIMIT, SWIGLU_LIMIT)
        act = glu * jax.nn.sigmoid(SWIGLU_ALPHA * glu) * (lin + 1)
        out = out + gates[:, e:e + 1] * (act @ w_down[e] + b_down[e])
    return out


def setup_inputs(seed: int = 0) -> dict:
    key = jax.random.key(seed)
    keys = jax.random.split(key, 48)
    count = [0]

    def nk():
        k = keys[count[0]]
        count[0] += 1
        return k

    def nrm(shape, scale):
        return scale * jax.random.normal(nk(), shape, jnp.float32)

    def unif(shape, lo, hi):
        return jax.random.uniform(nk(), shape, jnp.float32, lo, hi)

    D, F, E = D_MODEL, EXPERT_DIM, N_EXPERTS
    return {
        'x': nrm((BATCH, SEQ, D), 1.0),
        'c': nrm((BATCH, D), 1.0),
        'ctx': nrm((BATCH, CTX_LEN, D), 1.0),
        'c_ctx': nrm((D,), 1.0),
        'mod_w': nrm((DEPTH, D, N_MOD * D), 0.5 * D ** -0.5),
        'mod_b': nrm((DEPTH, N_MOD * D), 0.02),
        'ln_g': 1.0 + nrm((DEPTH, 2, D), 0.02),
        'ln_b': nrm((DEPTH, 2, D), 0.02),
        'ev_w_in': nrm((N_EVEN, D, EVEN_IN), D ** -0.5),
        'rw_mu': unif((N_EVEN, RW_IN), 0.2, 0.8),
        'rw_w0': unif((N_EVEN, 2, RW_DIM), -6.5, -1.5),
        'rw_w2': nrm((N_EVEN, 2, RW_DECAY_LORA, RW_DIM), 0.1 * RW_DECAY_LORA ** -0.5),
        'rw_a0': nrm((N_EVEN, 2, RW_DIM), 0.1),
        'rw_a2': nrm((N_EVEN, 2, RW_ICLR_LORA, RW_DIM), 0.1 * RW_ICLR_LORA ** -0.5),
        'rw_g2': nrm((N_EVEN, RW_GATE_LORA, RW_DIM), RW_GATE_LORA ** -0.5),
        'rw_k_k': 0.85 + nrm((N_EVEN, RW_DIM), 0.02),
        'rw_k_a': 1.0 + nrm((N_EVEN, RW_DIM), 0.02),
        'rw_r_k': nrm((N_EVEN, RW_HEADS, RW_HEAD_DIM), 0.1),
        'rw_gn_g': 1.0 + nrm((N_EVEN, RW_DIM), 0.02),
        'rw_gn_b': nrm((N_EVEN, RW_DIM), 0.02),
        'hg_lb': nrm((N_EVEN, 2, HG_DIM), 0.5),
        'hg_norm_g': 1.0 + nrm((N_EVEN, HG_VDIM), 0.02),
        'ev_w_out': nrm((N_EVEN, EVEN_OUT, D), DEEPNORM_BETA * EVEN_OUT ** -0.5),
        'od_w_in': nrm((N_ODD, D, MLA_IN), D ** -0.5),
        'mla_q_norm_g': 1.0 + nrm((N_ODD, MLA_Q_RANK), 0.02),
        'mla_w_qb': nrm((N_ODD, MLA_Q_RANK, MLA_HEADS * MLA_QK), MLA_Q_RANK ** -0.5),
        'mla_kv_norm_g': 1.0 + nrm((N_ODD, MLA_KV_RANK), 0.02),
        'mla_w_kvb': nrm((N_ODD, MLA_KV_RANK, MLA_HEADS * (MLA_NOPE + MLA_V)), MLA_KV_RANK ** -0.5),
        'od_w_out': nrm((N_ODD, MLA_HEADS * MLA_V, D), DEEPNORM_BETA * (MLA_HEADS * MLA_V) ** -0.5),
        'moe_w_router': nrm((DEPTH, D, E), D ** -0.5),
        'moe_b_router': nrm((DEPTH, E), 0.01),
        'moe_w_gu': nrm((DEPTH, E, D, 2 * F), D ** -0.5),
        'moe_b_gu': nrm((DEPTH, E, 2 * F), 0.02),
        'moe_w_down': nrm((DEPTH, E, F, D), DEEPNORM_BETA * F ** -0.5),
        'moe_b_down': nrm((DEPTH, E, D), 0.02),
    }


def reference(x, c, ctx, c_ctx, mod_w, mod_b, ln_g, ln_b, ev_w_in, rw_mu, rw_w0, rw_w2, rw_a0,
              rw_a2, rw_g2, rw_k_k, rw_k_a, rw_r_k, rw_gn_g, rw_gn_b, hg_lb, hg_norm_g, ev_w_out,
              od_w_in, mla_q_norm_g, mla_w_qb, mla_kv_norm_g, mla_w_kvb, od_w_out, moe_w_router,
              moe_b_router, moe_w_gu, moe_b_gu, moe_w_down, moe_b_down):
    b, n_lat, d = x.shape
    n_ctx = ctx.shape[1]
    rows = n_lat // GRID_W
    cos, sin = axial_rope_tables(rows)
    lb = jax.nn.softmax(hg_lb.astype(jnp.float32), axis=0)
    hg_lower = jnp.cumsum(lb, axis=0) - lb[0]
    c_act = jax.nn.silu(c)
    cc_act = jax.nn.silu(c_ctx)
    h_lat, h_ctx = x, ctx
    for layer in range(DEPTH):
        need_ctx = layer < DEPTH - 1
        j = layer // 2
        m_lat = jnp.split((c_act @ mod_w[layer] + mod_b[layer])[:, None, :], N_MOD, axis=-1)
        m_ctx = jnp.split(cc_act @ mod_w[layer] + mod_b[layer], N_MOD, axis=-1)
        u_lat = modulate(h_lat, m_lat[0], m_lat[1])
        u_ctx = modulate(h_ctx, m_ctx[0], m_ctx[1])
        if layer % 2 == 0:
            y_ctx, y_lat = rwkv7_hgrn2_mixer(
                u_ctx, u_lat, ev_w_in[j], rw_mu[j], rw_w0[j], rw_w2[j], rw_a0[j], rw_a2[j],
                rw_g2[j], rw_k_k[j], rw_k_a[j], rw_r_k[j], rw_gn_g[j], rw_gn_b[j],
                hg_lower[j], hg_norm_g[j], ev_w_out[j], need_ctx)
        else:
            y_ctx, y_lat = mla_mixer(
                u_ctx, u_lat, od_w_in[j], mla_q_norm_g[j], mla_w_qb[j], mla_kv_norm_g[j],
                mla_w_kvb[j], od_w_out[j], cos, sin, need_ctx)
        h_lat = layer_norm(DEEPNORM_ALPHA * h_lat + m_lat[2] * y_lat, ln_g[layer, 0], ln_b[layer, 0])
        u_lat = modulate(h_lat, m_lat[3], m_lat[4])
        moe_args = (moe_w_router[layer], moe_b_router[layer], moe_w_gu[layer], moe_b_gu[layer],
                    moe_w_down[layer], moe_b_down[layer])
        if need_ctx:
            h_ctx = layer_norm(DEEPNORM_ALPHA * h_ctx + m_ctx[2] * y_ctx, ln_g[layer, 0], ln_b[layer, 0])
            u_ctx = modulate(h_ctx, m_ctx[3], m_ctx[4])
            tokens = jnp.concatenate([u_ctx.reshape(-1, d), u_lat.reshape(-1, d)], axis=0)
            f = moe_clamped_swiglu(tokens, *moe_args)
            f_ctx = f[:b * n_ctx].reshape(h_ctx.shape)
            f_lat = f[b * n_ctx:].reshape(h_lat.shape)
            h_ctx = layer_norm(DEEPNORM_ALPHA * h_ctx + m_ctx[5] * f_ctx, ln_g[layer, 1], ln_b[layer, 1])
        else:
            f_lat = moe_clamped_swiglu(u_lat.reshape(-1, d), *moe_args).reshape(h_lat.shape)
        h_lat = layer_norm(DEEPNORM_ALPHA * h_lat + m_lat[5] * f_lat, ln_g[layer, 1], ln_b[layer, 1])
    return h_lat
```

```python
import functools

import jax
import jax.numpy as jnp
from jax import lax
from jax.experimental import pallas as pl
from jax.experimental.pallas import tpu as pltpu

F32 = jnp.float32
BF16 = jnp.bfloat16
HIGHEST = lax.Precision.HIGHEST

DEPTH = 4
GRID_W = 64
N_MOD = 6

RW_HEADS = 8
RW_HEAD_DIM = 64
RW_DIM = RW_HEADS * RW_HEAD_DIM
RW_DECAY_LORA = 64
RW_ICLR_LORA = 64
RW_GATE_LORA = 128
RW_GN_EPS = 64e-5
RW_IN = 3 * RW_DIM + 2 * RW_DECAY_LORA + 2 * RW_ICLR_LORA + RW_GATE_LORA

HG_HEADS = 4
HG_KEY_DIM = 128
HG_DIM = HG_HEADS * HG_KEY_DIM
HG_VDIM = HG_DIM
HG_F_FLOOR = 1e-30

MLA_HEADS = 16
MLA_NOPE = 64
MLA_ROPE = 32
MLA_V = 64
MLA_Q_RANK = 256
MLA_KV_RANK = 128
MLA_QK = MLA_NOPE + MLA_ROPE
MLA_SCALE = MLA_QK ** -0.5
ROPE_BASE = 10000.0

N_EXPERTS = 32
TOP_K = 4
EXPERT_DIM = 1024
SWIGLU_LIMIT = 7.0
SWIGLU_ALPHA = 1.702

DEEPNORM_ALPHA = (2 * DEPTH) ** 0.25
LN_EPS = 1e-5
RMS_EPS = 1e-6

SCAN_CHUNK = 64
HG_SUB = 16
MOE_TILE = 256
VMEM_LIMIT = 56 * 1024 * 1024


def _cparams(sem):
    return pltpu.CompilerParams(dimension_semantics=sem, vmem_limit_bytes=VMEM_LIMIT)


def _mm_kernel(x_ref, w_ref, o_ref):
    o_ref[...] = jnp.dot(x_ref[...].astype(BF16), w_ref[...],
                         preferred_element_type=F32).astype(o_ref.dtype)


def _row_tile(m, tm):
    if m <= tm:
        return m
    while m % tm:
        tm //= 2
    assert tm >= 8
    return tm


def matmul(x, w, out_dtype=F32, tm=512):
    m, k = x.shape
    n = w.shape[1]
    tm = _row_tile(m, tm)
    return pl.pallas_call(
        _mm_kernel,
        grid=(m // tm,),
        in_specs=[pl.BlockSpec((tm, k), lambda i: (i, 0)),
                  pl.BlockSpec((k, n), lambda i: (0, 0))],
        out_specs=pl.BlockSpec((tm, n), lambda i: (i, 0)),
        out_shape=jax.ShapeDtypeStruct((m, n), out_dtype),
        compiler_params=_cparams(("parallel",)),
    )(x, w.astype(BF16))


def _router_kernel(x_ref, w_ref, b_ref, o_ref):
    o_ref[...] = jnp.dot(x_ref[...], w_ref[...], preferred_element_type=F32,
                         precision=HIGHEST) + b_ref[...]


def router_logits(x, w, b, tm=512):
    m, k = x.shape
    n = w.shape[1]
    tm = _row_tile(m, tm)
    return pl.pallas_call(
        _router_kernel,
        grid=(m // tm,),
        in_specs=[pl.BlockSpec((tm, k), lambda i: (i, 0)),
                  pl.BlockSpec((k, n), lambda i: (0, 0)),
                  pl.BlockSpec((1, n), lambda i: (0, 0))],
        out_specs=pl.BlockSpec((tm, n), lambda i: (i, 0)),
        out_shape=jax.ShapeDtypeStruct((m, n), F32),
        compiler_params=_cparams(("parallel",)),
    )(x, w, b.reshape(1, n))


def _chunk_pos(c, d, n_ctx_chunks, n_chunks):
    bwd = jnp.where(c < n_ctx_chunks, n_ctx_chunks - 1 - c, n_chunks - 1 - (c - n_ctx_chunks))
    return jnp.where(d == 0, c, bwd)


def _scan_masks(d, n):
    row = lax.broadcasted_iota(jnp.int32, (n, n), 0)
    col = lax.broadcasted_iota(jnp.int32, (n, n), 1)
    lag = (row - col) * jnp.where(d == 0, 1, -1)
    incl = lag >= 0
    strict = lag > 0
    return incl, strict


def _dot(a, b):
    return jnp.dot(a.astype(BF16), b.astype(BF16), preferred_element_type=F32)


def _dot_nt(a, b):
    return lax.dot_general(a.astype(BF16), b.astype(BF16), (((1,), (1,)), ((), ())),
                           preferred_element_type=F32)


def _dot_tn(a, b):
    return lax.dot_general(a.astype(BF16), b.astype(BF16), (((0,), (0,)), ((), ())),
                           preferred_element_type=F32)


def _hgrn_kernel(q_ref, f_ref, i_ref, lo_ref, o_ref, st_ref):
    d = pl.program_id(1)
    c = pl.program_id(3)
    n = SCAN_CHUNK
    fwd = d == 0
    sgn = jnp.where(fwd, 1, -1)

    @pl.when(c == 0)
    def _():
        st_ref[...] = jnp.zeros_like(st_ref)

    lower = lo_ref[0]
    qraw = q_ref[0]
    sig = jax.nn.sigmoid(f_ref[0])
    f = lower + (1.0 - lower) * sig
    g = jnp.log(jnp.maximum(f, HG_F_FLOOR))
    kin = (1.0 - lower) * (1.0 - sig)
    q = qraw * jax.nn.sigmoid(qraw)
    v = i_ref[0]

    incl, _ = _scan_masks(d, n)
    cum = jnp.dot(incl.astype(F32), g, preferred_element_type=F32, precision=HIGHEST)
    cum_ex = cum - g
    tot = jnp.sum(g, axis=0, keepdims=True)

    st = st_ref[...]
    out = _dot_nt(q * jnp.exp(cum), st)

    col = lax.broadcasted_iota(jnp.int32, (HG_SUB, n), 1)
    row_l = lax.broadcasted_iota(jnp.int32, (HG_SUB, n), 0)
    outs = []
    for j in range(n // HG_SUB):
        lo, hi = j * HG_SUB, (j + 1) * HG_SUB
        q_j = q[lo:hi]
        cum_j = cum[lo:hi]
        ref_j = jnp.where(fwd, cum_ex[lo:lo + 1], cum_ex[hi - 1:hi])
        qt = q_j * jnp.exp(cum_j - ref_j)
        kt = kin * jnp.exp(jnp.minimum(ref_j - cum, 0.0))
        sc = _dot_nt(qt, kt)
        earlier = jnp.where(fwd, lo - 1 - col, col - hi) >= 0
        sc = jnp.where(earlier, sc, 0.0)
        row = row_l + lo
        for s in range(lo, hi):
            p = q_j * kin[s:s + 1] * jnp.exp(jnp.minimum(cum_j - cum[s:s + 1], 0.0))
            colsum = jnp.sum(p, axis=-1, keepdims=True)
            sc = jnp.where(col == s, jnp.where((row - s) * sgn >= 0, colsum, sc), sc)
        outs.append(_dot(sc, v))
    out = out + jnp.concatenate(outs, axis=0)
    o_ref[0, 0] = out

    kdec = kin * jnp.exp(tot - cum)
    st_ref[...] = st * jnp.exp(tot) + _dot_tn(v, kdec)


def hgrn2_scan(zb, lower, n_ctx):
    b, t, _ = zb.shape
    n = SCAN_CHUNK
    nc, ncc = t // n, n_ctx // n
    pos = functools.partial(_chunk_pos, n_ctx_chunks=ncc, n_chunks=nc)
    lo3 = lower.reshape(2 * HG_HEADS, 1, HG_KEY_DIM)
    return pl.pallas_call(
        _hgrn_kernel,
        grid=(b, 2, HG_HEADS, nc),
        in_specs=[
            pl.BlockSpec((1, n, 128), lambda bi, d, h, c: (bi, pos(c, d), h)),
            pl.BlockSpec((1, n, 128), lambda bi, d, h, c: (bi, pos(c, d), HG_HEADS + d * HG_HEADS + h)),
            pl.BlockSpec((1, n, 128), lambda bi, d, h, c: (bi, pos(c, d), 3 * HG_HEADS + h)),
            pl.BlockSpec((1, 1, 128), lambda bi, d, h, c: (d * HG_HEADS + h, 0, 0)),
        ],
        out_specs=pl.BlockSpec((1, 1, n, 128), lambda bi, d, h, c: (d, bi, pos(c, d), h)),
        out_shape=jax.ShapeDtypeStruct((2, b, t, HG_VDIM), F32),
        scratch_shapes=[pltpu.VMEM((HG_KEY_DIM, HG_KEY_DIM), F32)],
        compiler_params=_cparams(("parallel", "parallel", "parallel", "arbitrary")),
    )(zb, zb, zb, lo3)


def _rwkv_head(fwd, incl, strict, r, v, kk, lw, kd, bb, st):
    n = SCAN_CHUNK
    cum = jnp.dot(incl.astype(F32), lw, preferred_element_type=F32, precision=HIGHEST)
    tot = jnp.sum(lw, axis=0, keepdims=True)
    g_in = jnp.exp(cum)
    g_ex = jnp.exp(cum - lw)
    g_inv = jnp.exp(-cum)
    g_tail = jnp.exp(tot - cum)
    rt = r * g_in
    at = -kk * g_ex
    left = jnp.concatenate([at, rt], axis=0)
    right = jnp.concatenate([bb * g_inv, kd * g_inv], axis=0)
    p = _dot_nt(left, right)
    a_ab = jnp.where(strict, p[:n, :n], 0.0)
    a_ak = jnp.where(strict, p[:n, n:], 0.0)
    a_rb = jnp.where(incl, p[n:, :n], 0.0)
    a_rk = jnp.where(incl, p[n:, n:], 0.0)
    x = jnp.concatenate([at, _dot(a_ak, v)], axis=1)
    a = a_ab
    steps = n.bit_length() - 1
    for i in range(steps):
        x = x + _dot(a, x)
        if i + 1 < steps:
            a = _dot(a, a)
    w_mat = x[:, :RW_HEAD_DIM]
    u = _dot_nt(w_mat, st) + x[:, RW_HEAD_DIM:]
    y = _dot_nt(rt, st) + _dot(a_rb, u) + _dot(a_rk, v)
    uv = jnp.concatenate([u, v], axis=0)
    bk = jnp.concatenate([bb * g_tail, kd * g_tail], axis=0)
    st_new = st * jnp.exp(tot) + _dot_tn(uv, bk)
    return y, st_new


def _rwkv_kernel(r_ref, v_ref, kk_ref, lw_ref, kd_ref, bb_ref, y_ref, st_ref):
    d = pl.program_id(1)
    c = pl.program_id(2)

    @pl.when(c == 0)
    def _():
        st_ref[...] = jnp.zeros_like(st_ref)

    incl, strict = _scan_masks(d, SCAN_CHUNK)
    fwd = d == 0
    for h in range(RW_HEADS):
        y, st_new = _rwkv_head(fwd, incl, strict, r_ref[0, h], v_ref[0, h], kk_ref[0, h],
                               lw_ref[0, 0, h], kd_ref[0, 0, h], bb_ref[0, 0, h], st_ref[h])
        y_ref[0, 0, h] = y
        st_ref[h] = st_new


def rwkv7_scan(r, v, kk, lw, kd, bb, n_ctx):
    b, h, t, nd = r.shape
    n = SCAN_CHUNK
    nc, ncc = t // n, n_ctx // n
    pos = functools.partial(_chunk_pos, n_ctx_chunks=ncc, n_chunks=nc)
    shared = pl.BlockSpec((1, h, n, nd), lambda bi, d, c: (bi, 0, pos(c, d), 0))
    per_dir = pl.BlockSpec((1, 1, h, n, nd), lambda bi, d, c: (d, bi, 0, pos(c, d), 0))
    return pl.pallas_call(
        _rwkv_kernel,
        grid=(b, 2, nc),
        in_specs=[shared, shared, shared, per_dir, per_dir, per_dir],
        out_specs=per_dir,
        out_shape=jax.ShapeDtypeStruct((2, b, h, t, nd), F32),
        scratch_shapes=[pltpu.VMEM((h, nd, nd), F32)],
        compiler_params=_cparams(("parallel", "parallel", "arbitrary")),
    )(r, v, kk, lw, kd, bb)


def _attn_kernel(q_ref, k_ref, v_ref, o_ref):
    s = _dot_nt(q_ref[0, 0], k_ref[0, 0]) * MLA_SCALE
    m = jnp.max(s, axis=-1, keepdims=True)
    p = jnp.exp(s - m)
    l = jnp.sum(p, axis=-1, keepdims=True)
    o_ref[0, 0] = jnp.dot(p.astype(BF16), v_ref[0, 0], preferred_element_type=F32) / l


def attention(q, k, v, tq=256):
    b, h, nq, dk = q.shape
    nk, dv = k.shape[2], v.shape[3]
    tq = min(tq, nq)
    return pl.pallas_call(
        _attn_kernel,
        grid=(b, h, nq // tq),
        in_specs=[pl.BlockSpec((1, 1, tq, dk), lambda bi, hi, qi: (bi, hi, qi, 0)),
                  pl.BlockSpec((1, 1, nk, dk), lambda bi, hi, qi: (bi, hi, 0, 0)),
                  pl.BlockSpec((1, 1, nk, dv), lambda bi, hi, qi: (bi, hi, 0, 0))],
        out_specs=pl.BlockSpec((1, 1, tq, dv), lambda bi, hi, qi: (bi, hi, qi, 0)),
        out_shape=jax.ShapeDtypeStruct((b, h, nq, dv), F32),
        compiler_params=_cparams(("parallel", "parallel", "parallel")),
    )(q, k, v)


def _moe_kernel(te_ref, nt_ref, x_ref, wgu_ref, bgu_ref, wd_ref, bd_ref, o_ref):
    i = pl.program_id(0)

    @pl.when(i < nt_ref[0])
    def _():
        h = jnp.dot(x_ref[...], wgu_ref[0], preferred_element_type=F32) + bgu_ref[0]
        glu = jnp.minimum(h[:, :EXPERT_DIM], SWIGLU_LIMIT)
        lin = jnp.clip(h[:, EXPERT_DIM:], -SWIGLU_LIMIT, SWIGLU_LIMIT)
        act = glu * jax.nn.sigmoid(SWIGLU_ALPHA * glu) * (lin + 1.0)
        o_ref[...] = jnp.dot(act.astype(BF16), wd_ref[0], preferred_element_type=F32) + bd_ref[0]

    @pl.when(i >= nt_ref[0])
    def _():
        o_ref[...] = jnp.zeros_like(o_ref)


def moe_grouped(x_sorted, tile_expert, n_tiles_used, w_gu, b_gu, w_down, b_down):
    p, dm = x_sorted.shape
    tm = MOE_TILE
    e, _, f2 = w_gu.shape
    grid_spec = pltpu.PrefetchScalarGridSpec(
        num_scalar_prefetch=2,
        grid=(p // tm,),
        in_specs=[
            pl.BlockSpec((tm, dm), lambda i, te, nt: (i, 0)),
            pl.BlockSpec((1, dm, f2), lambda i, te, nt: (te[i], 0, 0)),
            pl.BlockSpec((1, 1, f2), lambda i, te, nt: (te[i], 0, 0)),
            pl.BlockSpec((1, f2 // 2, dm), lambda i, te, nt: (te[i], 0, 0)),
            pl.BlockSpec((1, 1, dm), lambda i, te, nt: (te[i], 0, 0)),
        ],
        out_specs=pl.BlockSpec((tm, dm), lambda i, te, nt: (i, 0)),
    )
    return pl.pallas_call(
        _moe_kernel,
        grid_spec=grid_spec,
        out_shape=jax.ShapeDtypeStruct((p, dm), F32),
        compiler_params=_cparams(("arbitrary",)),
    )(tile_expert, n_tiles_used, x_sorted, w_gu, b_gu.reshape(e, 1, f2), w_down,
      b_down.reshape(e, 1, dm))


def moe(u, w_router, b_router, w_gu, b_gu, w_down, b_down):
    n, dm = u.shape
    tm = MOE_TILE
    logits = router_logits(u, w_router, b_router)
    top_logit, top_idx = lax.top_k(logits, TOP_K)
    gates = jax.nn.softmax(top_logit, axis=-1)
    e_flat = top_idx.reshape(-1).astype(jnp.int32)
    npair = n * TOP_K
    order = jnp.argsort(e_flat, stable=True).astype(jnp.int32)
    e_sorted = e_flat[order]
    counts = jnp.zeros((N_EXPERTS,), jnp.int32).at[e_flat].add(1)
    padded = ((counts + tm - 1) // tm) * tm
    ends_p = jnp.cumsum(padded)
    starts_p = ends_p - padded
    starts = jnp.cumsum(counts) - counts
    dest = starts_p[e_sorted] + jnp.arange(npair, dtype=jnp.int32) - starts[e_sorted]
    p_rows = npair + N_EXPERTS * tm
    src_tok = jnp.zeros((p_rows,), jnp.int32).at[dest].set(order // TOP_K)
    pos = jnp.zeros((npair,), jnp.int32).at[order].set(dest)
    n_tiles = p_rows // tm
    tile_expert = jnp.minimum(
        jnp.searchsorted(ends_p, jnp.arange(n_tiles, dtype=jnp.int32) * tm, side='right'),
        N_EXPERTS - 1).astype(jnp.int32)
    n_used = (ends_p[-1] // tm).astype(jnp.int32).reshape(1)
    x_sorted = u.astype(BF16)[src_tok]
    y_sorted = moe_grouped(x_sorted, tile_expert, n_used, w_gu.astype(BF16), b_gu,
                           w_down.astype(BF16), b_down)
    y = y_sorted[pos].reshape(n, TOP_K, dm)
    return jnp.sum(y * gates[:, :, None], axis=1)


def _layer_norm(x, g, b):
    mu = jnp.mean(x, -1, keepdims=True)
    var = jnp.mean(jnp.square(x - mu), -1, keepdims=True)
    return (x - mu) * lax.rsqrt(var + LN_EPS) * g + b


def _rms_normalize(x):
    return x * lax.rsqrt(jnp.mean(jnp.square(x), -1, keepdims=True) + RMS_EPS)


def _seg_shift(z, n_ctx):
    t = z.shape[1]
    idx = jnp.arange(t)[None, :, None]
    prev = jnp.pad(z[:, :-1], ((0, 0), (1, 0), (0, 0)))
    nxt = jnp.pad(z[:, 1:], ((0, 0), (0, 1), (0, 0)))
    prev = jnp.where(idx == n_ctx, 0.0, prev)
    nxt = jnp.where(idx == n_ctx - 1, 0.0, nxt)
    return 0.5 * (prev + nxt) - z


def _even_mixer(u, n_ctx, w_in, mu, w0, w2, a0, a2, g2, k_k, k_a, r_k, gn_g, gn_b, lower,
                hg_norm_g, w_out):
    b, t, dm = u.shape
    z = matmul(u.reshape(b * t, dm), w_in).reshape(b, t, -1)
    za, zb = z[..., :RW_IN], z[..., RW_IN:]
    za = za + mu * _seg_shift(za, n_ctx)

    c = RW_DIM
    r, k, v = za[..., :c], za[..., c:2 * c], za[..., 2 * c:3 * c]
    o = 3 * c
    wd = za[..., o:o + 2 * RW_DECAY_LORA].reshape(b, t, 2, RW_DECAY_LORA)
    o += 2 * RW_DECAY_LORA
    ad = za[..., o:o + 2 * RW_ICLR_LORA].reshape(b, t, 2, RW_ICLR_LORA)
    o += 2 * RW_ICLR_LORA
    gd = za[..., o:]
    w_pre = w0 + jnp.einsum('btdr,drc->btdc', jnp.tanh(wd), w2)
    lw = -jnp.exp(-0.5) * jax.nn.sigmoid(w_pre)
    a = jax.nn.sigmoid(a0 + jnp.einsum('btdr,drc->btdc', ad, a2))
    g = jax.nn.sigmoid(gd) @ g2
    to_heads = lambda x: x.reshape(*x.shape[:-1], RW_HEADS, RW_HEAD_DIM)
    kk = to_heads(k * k_k)
    kk = kk * lax.rsqrt(jnp.maximum(jnp.sum(jnp.square(kk), -1, keepdims=True), 1e-24))
    kk = kk.reshape(b, t, c)
    k_dir = k[:, :, None] * (1 + (a - 1) * k_a)
    bb = kk[:, :, None] * a
    hm = lambda x: jnp.transpose(to_heads(x), (0, 2, 1, 3))
    hm2 = lambda x: jnp.transpose(to_heads(x), (2, 0, 3, 1, 4))
    y = rwkv7_scan(hm(r), hm(v), hm(kk), hm2(lw), hm2(k_dir), hm2(bb), n_ctx)
    y = jnp.transpose(y[0] + y[1], (0, 2, 1, 3))
    mean = jnp.mean(y, -1, keepdims=True)
    var = jnp.mean(jnp.square(y - mean), -1, keepdims=True)
    y = ((y - mean) * lax.rsqrt(var + RW_GN_EPS)).reshape(b, t, c) * gn_g + gn_b
    bonus = jnp.sum(to_heads(r) * to_heads(k_dir[:, :, 0] + k_dir[:, :, 1]) * r_k, -1, keepdims=True)
    y_rw = (y + (bonus * to_heads(v)).reshape(b, t, c)) * g

    o_hg = hgrn2_scan(zb, lower, n_ctx)
    o_hg = (o_hg[0] + o_hg[1]).reshape(b, t, HG_HEADS, HG_KEY_DIM)
    o_hg = _rms_normalize(o_hg).reshape(b, t, HG_VDIM) * hg_norm_g
    g_hg = zb[..., 4 * HG_DIM:]
    y_hg = o_hg * (g_hg * jax.nn.sigmoid(g_hg))

    y_all = jnp.concatenate([y_rw, y_hg], axis=-1)
    return matmul(y_all.reshape(b * t, -1), w_out).reshape(b, t, dm)


def _rope_tables(rows):
    t = jnp.arange(rows * GRID_W)
    row = (t // GRID_W).astype(F32)
    col = (t % GRID_W).astype(F32)
    half = MLA_ROPE // 2
    inv_freq = ROPE_BASE ** (-jnp.arange(0, half, 2, dtype=F32) / half)
    ang_r = row[:, None] * inv_freq
    ang_c = col[:, None] * inv_freq
    ang = jnp.concatenate([ang_r, ang_r, ang_c, ang_c], axis=-1)
    return jnp.cos(ang), jnp.sin(ang)


def _rope(x, cos, sin):
    r1, r2, c1, c2 = jnp.split(x, 4, axis=-1)
    rot = jnp.concatenate([-r2, r1, -c2, c1], axis=-1)
    return x * cos + rot * sin


def _mla_mixer(u, n_ctx, w_in, q_norm_g, w_qb, kv_norm_g, w_kvb, w_out, cos, sin, need_ctx):
    b, t, dm = u.shape
    z = matmul(u.reshape(b * t, dm), w_in)
    zq = z[:, :MLA_Q_RANK]
    kva = z[:, MLA_Q_RANK:MLA_Q_RANK + MLA_KV_RANK]
    kpe = z[:, MLA_Q_RANK + MLA_KV_RANK:].reshape(b, t, MLA_ROPE)
    q = matmul(_rms_normalize(zq) * q_norm_g, w_qb).reshape(b, t, MLA_HEADS, MLA_QK)
    kv = matmul(_rms_normalize(kva) * kv_norm_g, w_kvb).reshape(b, t, MLA_HEADS, MLA_NOPE + MLA_V)
    q_rope = jnp.concatenate([q[:, :n_ctx, :, MLA_NOPE:],
                              _rope(q[:, n_ctx:, :, MLA_NOPE:], cos[:, None], sin[:, None])], axis=1)
    q = jnp.concatenate([q[..., :MLA_NOPE], q_rope], axis=-1)
    kpe = jnp.concatenate([kpe[:, :n_ctx], _rope(kpe[:, n_ctx:], cos, sin)], axis=1)
    k = jnp.concatenate([kv[..., :MLA_NOPE],
                         jnp.broadcast_to(kpe[:, :, None], (b, t, MLA_HEADS, MLA_ROPE))], axis=-1)
    v = kv[..., MLA_NOPE:]
    hm = lambda x: jnp.transpose(x, (0, 2, 1, 3)).astype(BF16)
    qh, kh, vh = hm(q), hm(k), hm(v)
    o_lat = attention(qh[:, :, n_ctx:], kh, vh)
    if need_ctx:
        o_ctx = attention(qh[:, :, :n_ctx], kh[:, :, :n_ctx], vh[:, :, :n_ctx])
        o = jnp.concatenate([o_ctx, o_lat], axis=2)
    else:
        o = jnp.concatenate([jnp.zeros((b, MLA_HEADS, n_ctx, MLA_V), F32), o_lat], axis=2)
    o = jnp.transpose(o, (0, 2, 1, 3)).reshape(b * t, MLA_HEADS * MLA_V)
    return matmul(o, w_out).reshape(b, t, dm)


def kernel(x, c, ctx, c_ctx, mod_w, mod_b, ln_g, ln_b, ev_w_in, rw_mu, rw_w0, rw_w2, rw_a0, rw_a2,
           rw_g2, rw_k_k, rw_k_a, rw_r_k, rw_gn_g, rw_gn_b, hg_lb, hg_norm_g, ev_w_out, od_w_in,
           mla_q_norm_g, mla_w_qb, mla_kv_norm_g, mla_w_kvb, od_w_out, moe_w_router, moe_b_router,
           moe_w_gu, moe_b_gu, moe_w_down, moe_b_down):
    b, n_lat, dm = x.shape
    n_ctx = ctx.shape[1]
    t = n_ctx + n_lat
    cos, sin = _rope_tables(n_lat // GRID_W)
    lb = jax.nn.softmax(hg_lb.astype(F32), axis=0)
    hg_lower = jnp.cumsum(lb, axis=0) - lb[0]
    c_act = c * jax.nn.sigmoid(c)
    cc_act = c_ctx * jax.nn.sigmoid(c_ctx)
    mod_in = jnp.concatenate([c_act, cc_act[None]], axis=0)
    pad = (-mod_in.shape[0]) % 8
    mod_in = jnp.pad(mod_in, ((0, pad), (0, 0)))
    is_ctx = (jnp.arange(t) < n_ctx)[None, :, None]
    h = jnp.concatenate([ctx, x], axis=1)
    for layer in range(DEPTH):
        need_ctx = layer < DEPTH - 1
        j = layer // 2
        mods = matmul(mod_in, mod_w[layer]) + mod_b[layer]
        m_lat = mods[:b].reshape(b, 1, N_MOD, dm)
        m_ctx = jnp.broadcast_to(mods[b].reshape(1, 1, N_MOD, dm), (b, 1, N_MOD, dm))
        m = [jnp.where(is_ctx, m_ctx[:, :, i], m_lat[:, :, i]) for i in range(N_MOD)]
        u = h * (1 + m[1]) + m[0]
        if layer % 2 == 0:
            y = _even_mixer(u, n_ctx, ev_w_in[j], rw_mu[j], rw_w0[j], rw_w2[j], rw_a0[j], rw_a2[j],
                            rw_g2[j], rw_k_k[j], rw_k_a[j], rw_r_k[j], rw_gn_g[j], rw_gn_b[j],
                            hg_lower[j], hg_norm_g[j], ev_w_out[j])
        else:
            y = _mla_mixer(u, n_ctx, od_w_in[j], mla_q_norm_g[j], mla_w_qb[j], mla_kv_norm_g[j],
                           mla_w_kvb[j], od_w_out[j], cos, sin, need_ctx)
        h = _layer_norm(DEEPNORM_ALPHA * h + m[2] * y, ln_g[layer, 0], ln_b[layer, 0])
        u = h * (1 + m[4]) + m[3]
        f = moe(u.reshape(b * t, dm), moe_w_router[layer], moe_b_router[layer], moe_w_gu[layer],
                moe_b_gu[layer], moe_w_down[layer], moe_b_down[layer]).reshape(b, t, dm)
        h = _layer_norm(DEEPNORM_ALPHA * h + m[5] * f, ln_g[layer, 1], ln_b[layer, 1])
    return h[:, n_ctx:]
```

```python
import functools

import jax
import jax.numpy as jnp
from jax import lax
from jax.experimental import pallas as pl
from jax.experimental.pallas import tpu as pltpu

F32 = jnp.float32
BF16 = jnp.bfloat16
HIGHEST = lax.Precision.HIGHEST

DEPTH = 4
GRID_W = 64
N_MOD = 6

RW_HEADS = 8
RW_HEAD_DIM = 64
RW_DIM = RW_HEADS * RW_HEAD_DIM
RW_DECAY_LORA = 64
RW_ICLR_LORA = 64
RW_GATE_LORA = 128
RW_GN_EPS = 64e-5
RW_IN = 3 * RW_DIM + 2 * RW_DECAY_LORA + 2 * RW_ICLR_LORA + RW_GATE_LORA
RW_LORA_OFF = 3 * RW_DIM

HG_HEADS = 4
HG_KEY_DIM = 128
HG_DIM = HG_HEADS * HG_KEY_DIM
HG_VDIM = HG_DIM
HG_IN = 3 * HG_DIM + 2 * HG_VDIM
HG_F_FLOOR = 1e-30

MLA_HEADS = 16
MLA_NOPE = 64
MLA_ROPE = 32
MLA_V = 64
MLA_Q_RANK = 256
MLA_KV_RANK = 128
MLA_QK = MLA_NOPE + MLA_ROPE
MLA_SCALE = MLA_QK ** -0.5
ROPE_BASE = 10000.0

N_EXPERTS = 32
TOP_K = 4
EXPERT_DIM = 1024
SWIGLU_LIMIT = 7.0
SWIGLU_ALPHA = 1.702

DEEPNORM_ALPHA = (2 * DEPTH) ** 0.25
LN_EPS = 1e-5
RMS_EPS = 1e-6

LANES = 128
SCAN_CHUNK = 64
HG_SUB = 16
MOE_TILE = 256
RANK_BLOCK = 256
VMEM_LIMIT = 56 * 1024 * 1024


def _cparams(sem):
    return pltpu.CompilerParams(dimension_semantics=sem, vmem_limit_bytes=VMEM_LIMIT)


def _row_tile(m, tm):
    if m <= tm:
        return m
    while m % tm:
        tm //= 2
    assert tm >= 8
    return tm


def _mm_kernel(x_ref, w_ref, *o_refs):
    acc = jnp.dot(x_ref[...].astype(BF16), w_ref[...], preferred_element_type=F32)
    off = 0
    for o_ref in o_refs:
        width = o_ref.shape[1]
        o_ref[...] = acc[:, off:off + width].astype(o_ref.dtype)
        off += width


def matmul(x, w, splits=None, out_dtype=F32, tm=512):
    m, k = x.shape
    n = w.shape[1]
    tm = _row_tile(m, tm)
    widths = (n,) if splits is None else splits
    assert sum(widths) == n
    outs = pl.pallas_call(
        _mm_kernel,
        grid=(m // tm,),
        in_specs=[pl.BlockSpec((tm, k), lambda i: (i, 0)),
                  pl.BlockSpec((k, n), lambda i: (0, 0))],
        out_specs=[pl.BlockSpec((tm, wd), lambda i: (i, 0)) for wd in widths],
        out_shape=[jax.ShapeDtypeStruct((m, wd), out_dtype) for wd in widths],
        compiler_params=_cparams(("parallel",)),
    )(x, w.astype(BF16))
    return outs[0] if splits is None else outs


def _router_kernel(x_ref, w_ref, b_ref, o_ref):
    o_ref[...] = jnp.dot(x_ref[...], w_ref[...], preferred_element_type=F32,
                         precision=HIGHEST) + b_ref[...]


def router_logits(x, w, b, tm=512):
    m, k = x.shape
    n = w.shape[1]
    tm = _row_tile(m, tm)
    return pl.pallas_call(
        _router_kernel,
        grid=(m // tm,),
        in_specs=[pl.BlockSpec((tm, k), lambda i: (i, 0)),
                  pl.BlockSpec((k, n), lambda i: (0, 0)),
                  pl.BlockSpec((1, n), lambda i: (0, 0))],
        out_specs=pl.BlockSpec((tm, n), lambda i: (i, 0)),
        out_shape=jax.ShapeDtypeStruct((m, n), F32),
        compiler_params=_cparams(("parallel",)),
    )(x, w, b.reshape(1, n))


def _bwd_chunk(c, n_ctx_chunks, n_chunks):
    return jnp.where(c < n_ctx_chunks, n_ctx_chunks - 1 - c, n_chunks - 1 - (c - n_ctx_chunks))


def _dot(a, b):
    return jnp.dot(a.astype(BF16), b.astype(BF16), preferred_element_type=F32)


def _dot_nt(a, b):
    return lax.dot_general(a.astype(BF16), b.astype(BF16), (((1,), (1,)), ((), ())),
                           preferred_element_type=F32)


def _dot_tn(a, b):
    return lax.dot_general(a.astype(BF16), b.astype(BF16), (((0,), (0,)), ((), ())),
                           preferred_element_type=F32)


def _dot_f32(a, b):
    return jnp.dot(a, b, preferred_element_type=F32, precision=HIGHEST)


def _seg_sum(x, ones_bd):
    hi = x.astype(BF16)
    lo = (x - hi.astype(F32)).astype(BF16)
    return (jnp.dot(hi, ones_bd, preferred_element_type=F32)
            + jnp.dot(lo, ones_bd, preferred_element_type=F32))


def _block_diag_ones(width, block):
    i = jnp.arange(width) // block
    return (i[:, None] == i[None, :]).astype(BF16)


def _hgrn_kernel(zf_ref, zb_ref, lo_ref, bd_ref, of_ref, ob_ref, st_ref):
    c = pl.program_id(1)
    n, sub = SCAN_CHUNK, HG_SUB
    nb = n // sub

    @pl.when(c == 0)
    def _():
        st_ref[...] = jnp.zeros_like(st_ref)

    row = lax.broadcasted_iota(jnp.int32, (n, n), 0)
    col = lax.broadcasted_iota(jnp.int32, (n, n), 1)
    colb = lax.broadcasted_iota(jnp.int32, (sub, LANES), 1)
    rown = lax.broadcasted_iota(jnp.int32, (sub, n), 0)
    coln = lax.broadcasted_iota(jnp.int32, (sub, n), 1)
    bd2 = bd_ref[...]

    dirs = []
    for d, z_ref in ((0, zf_ref), (1, zb_ref)):
        z = z_ref[0]
        lower = lo_ref[d]
        qraw = z[:, :HG_DIM]
        sig = jax.nn.sigmoid(z[:, HG_DIM * (1 + d):HG_DIM * (2 + d)])
        v = z[:, 3 * HG_DIM:3 * HG_DIM + HG_VDIM]
        f = lower + (1.0 - lower) * sig
        g = jnp.log(jnp.maximum(f, HG_F_FLOOR))
        kin = (1.0 - lower) * (1.0 - sig)
        q = qraw * jax.nn.sigmoid(qraw)
        lag = (row - col) if d == 0 else (col - row)
        cum = _dot_f32((lag >= 0).astype(F32), g)
        cum_ex = cum - g
        tot = jnp.sum(g, axis=0, keepdims=True)
        blocks = []
        for j in range(nb):
            lo, hi = j * sub, (j + 1) * sub
            q_j, cum_j = q[lo:hi], cum[lo:hi]
            ref_j = cum_ex[lo:lo + 1] if d == 0 else cum_ex[hi - 1:hi]
            qt = q_j * jnp.exp(cum_j - ref_j)
            kt = kin * jnp.exp(jnp.minimum(ref_j - cum, 0.0))
            pair = jnp.concatenate(
                [q_j * kin[s:s + 1] * jnp.exp(jnp.minimum(cum_j - cum[s:s + 1], 0.0))
                 for s in range(lo, hi)], axis=0)
            blocks.append((qt, kt, pair))
        dirs.append(dict(d=d, v=v, qe=q * jnp.exp(cum), kdec=kin * jnp.exp(tot - cum),
                         etot=jnp.exp(tot), blocks=blocks))

    for dd, o_ref in zip(dirs, (of_ref, ob_ref)):
        d = dd['d']
        for j, (qt, kt, pair) in enumerate(dd['blocks']):
            lo, hi = j * sub, (j + 1) * sub
            sums = [_dot(pair[:, 2 * LANES * hp:2 * LANES * (hp + 1)], bd2) for hp in range(HG_HEADS // 2)]
            for h in range(HG_HEADS):
                hs = slice(LANES * h, LANES * (h + 1))
                rs = sums[h // 2][:, LANES * (h % 2):LANES * (h % 2 + 1)]
                diag = jnp.zeros((sub, LANES), F32)
                for s in range(sub):
                    diag = jnp.where(colb == lo + s, rs[s * sub:(s + 1) * sub], diag)
                off = _dot_nt(qt[:, hs], kt[:, hs])
                if d == 0:
                    earlier, ordered = coln < lo, rown + lo >= coln
                else:
                    earlier, ordered = coln >= hi, rown + lo <= coln
                sc = jnp.where(earlier, off, jnp.where(ordered, diag[:, :n], 0.0))
                dd.setdefault(('out', h), []).append(_dot(sc, dd['v'][:, hs]))

    for dd, o_ref in zip(dirs, (of_ref, ob_ref)):
        d = dd['d']
        for h in range(HG_HEADS):
            hs = slice(LANES * h, LANES * (h + 1))
            st = st_ref[d * HG_HEADS + h]
            out = jnp.concatenate(dd[('out', h)], axis=0) + _dot_nt(dd['qe'][:, hs], st)
            o_ref[0, :, hs] = out
            st_ref[d * HG_HEADS + h] = (st * dd['etot'][:, hs]
                                        + _dot_tn(dd['v'][:, hs], dd['kdec'][:, hs]))


def hgrn2_scan(zb, lower, n_ctx):
    b, t, _ = zb.shape
    n = SCAN_CHUNK
    nc, ncc = t // n, n_ctx // n
    bwd = functools.partial(_bwd_chunk, n_ctx_chunks=ncc, n_chunks=nc)
    const = lambda shape: pl.BlockSpec(shape, lambda bi, c: (0,) * len(shape))
    return pl.pallas_call(
        _hgrn_kernel,
        grid=(b, nc),
        in_specs=[pl.BlockSpec((1, n, HG_IN), lambda bi, c: (bi, c, 0)),
                  pl.BlockSpec((1, n, HG_IN), lambda bi, c: (bi, bwd(c), 0)),
                  const((2, 1, HG_DIM)), const((2 * LANES, 2 * LANES))],
        out_specs=[pl.BlockSpec((1, n, HG_VDIM), lambda bi, c: (bi, c, 0)),
                   pl.BlockSpec((1, n, HG_VDIM), lambda bi, c: (bi, bwd(c), 0))],
        out_shape=[jax.ShapeDtypeStruct((b, t, HG_VDIM), F32)] * 2,
        scratch_shapes=[pltpu.VMEM((2 * HG_HEADS, HG_KEY_DIM, HG_KEY_DIM), F32)],
        compiler_params=_cparams(("parallel", "arbitrary")),
    )(zb, zb, lower.reshape(2, 1, HG_DIM), _block_diag_ones(2 * LANES, LANES))


def _rwkv_iclr(z, d, a0_ref, a2_ref):
    ad = z[:, RW_LORA_OFF + LANES:RW_LORA_OFF + 2 * LANES]
    return jax.nn.sigmoid(a0_ref[d] + _dot(ad, a2_ref[d]))


def _rwkv_scan_kernel(zf_ref, zb_ref, w0_ref, w2_ref, a0_ref, a2_ref, kk_ref, ka_ref, bd_ref,
                      yf_ref, yb_ref, st_ref):
    c = pl.program_id(1)
    n, cdim = SCAN_CHUNK, RW_DIM
    npair = cdim // LANES

    @pl.when(c == 0)
    def _():
        st_ref[...] = jnp.zeros_like(st_ref)

    row = lax.broadcasted_iota(jnp.int32, (n, n), 0)
    col = lax.broadcasted_iota(jnp.int32, (n, n), 1)
    row2 = lax.broadcasted_iota(jnp.int32, (n, 2 * n), 0)
    col2 = lax.broadcasted_iota(jnp.int32, (n, 2 * n), 1) % n
    lane = lax.broadcasted_iota(jnp.int32, (1, LANES), 1)
    half = [lane < RW_HEAD_DIM, lane >= RW_HEAD_DIM]
    first2 = lax.broadcasted_iota(jnp.int32, (1, 2 * LANES), 1) % LANES < RW_HEAD_DIM
    r2 = lax.broadcasted_iota(jnp.int32, (LANES, LANES), 0) < RW_HEAD_DIM
    c2 = lax.broadcasted_iota(jnp.int32, (LANES, LANES), 1) < RW_HEAD_DIM
    same_head = r2 == c2

    chains = []
    for d, z_ref in ((0, zf_ref), (1, zb_ref)):
        z = z_ref[0]
        r, k, v = z[:, :cdim], z[:, cdim:2 * cdim], z[:, 2 * cdim:3 * cdim]
        wd = z[:, RW_LORA_OFF:RW_LORA_OFF + LANES]
        w_pre = w0_ref[d] + _dot(jnp.tanh(wd), w2_ref[d])
        lw = -jnp.exp(-0.5) * jax.nn.sigmoid(w_pre)
        a = _rwkv_iclr(z, d, a0_ref, a2_ref)
        kk = k * kk_ref[...]
        kk = kk * lax.rsqrt(jnp.maximum(_seg_sum(kk * kk, bd_ref[...]), 1e-24))
        kd = k * (1.0 + (a - 1.0) * ka_ref[...])
        bb = kk * a
        lag = (row - col) if d == 0 else (col - row)
        lag2 = (row2 - col2) if d == 0 else (col2 - row2)
        incl2, strict2 = lag2 >= 0, lag2 > 0
        cum = _dot_f32((lag >= 0).astype(F32), lw)
        tot = jnp.sum(lw, axis=0, keepdims=True)
        g_inv = jnp.exp(-cum)
        g_tail = jnp.exp(tot - cum)
        rt = r * jnp.exp(cum)
        at = -kk * jnp.exp(cum - lw)
        etot = jnp.exp(tot)
        left = jnp.concatenate([at, rt], axis=0)
        right = jnp.concatenate([bb * g_inv, kd * g_inv], axis=0)
        tail = jnp.concatenate([bb * g_tail, kd * g_tail], axis=0)
        for p in range(npair):
            ps = slice(LANES * p, LANES * (p + 1))
            chains.append(dict(d=d, p=p, ps=ps, incl2=incl2, strict2=strict2, left=left[:, ps],
                               right=right[:, ps], tail=tail[:, ps], at=at[:, ps], rt=rt[:, ps],
                               v=v[:, ps], etot=etot[:, ps]))

    for ch in chains:
        ch['top'], ch['bot'] = [], []
        for hh in range(2):
            prod = _dot_nt(ch['left'], jnp.where(half[hh], ch['right'], 0.0))
            ch['top'].append(jnp.where(ch['strict2'], prod[:n], 0.0))
            ch['bot'].append(jnp.where(ch['incl2'], prod[n:], 0.0))
    for ch in chains:
        av = [_dot(ch['top'][hh][:, n:], ch['v']) for hh in range(2)]
        x0 = jnp.concatenate([ch['at'], jnp.where(half[0], av[0], av[1])], axis=1)
        ch['x'] = [x0, x0]
        ch['a'] = [ch['top'][hh][:, :n] for hh in range(2)]
    steps = n.bit_length() - 1
    for i in range(steps):
        for ch in chains:
            for hh in range(2):
                ch['x'][hh] = ch['x'][hh] + _dot(ch['a'][hh], ch['x'][hh])
                if i + 1 < steps:
                    ch['a'][hh] = _dot(ch['a'][hh], ch['a'][hh])
    for ch in chains:
        x = jnp.where(first2, ch['x'][0], ch['x'][1])
        st = st_ref[ch['d'] * npair + ch['p']]
        ws = _dot_nt(jnp.concatenate([x[:, :LANES], ch['rt']], axis=0), st)
        u = ws[:n] + x[:, LANES:]
        uv = jnp.concatenate([u, ch['v']], axis=0)
        y = ws[n:] + jnp.where(half[0], _dot(ch['bot'][0], uv), _dot(ch['bot'][1], uv))
        y_ref = yf_ref if ch['d'] == 0 else yb_ref
        y_ref[0, :, ch['ps']] = y
        delta = _dot_tn(uv, ch['tail'])
        st_ref[ch['d'] * npair + ch['p']] = st * ch['etot'] + jnp.where(same_head, delta, 0.0)


def _pad_lora(w, d):
    return jnp.concatenate([w[d] if i == d else jnp.zeros_like(w[i]) for i in range(2)], axis=0)


def rwkv7_scan(za, w0, w2, a0, a2, k_k, k_a, n_ctx):
    b, t, _ = za.shape
    n, cdim = SCAN_CHUNK, RW_DIM
    nc, ncc = t // n, n_ctx // n
    bwd = functools.partial(_bwd_chunk, n_ctx_chunks=ncc, n_chunks=nc)
    const = lambda shape: pl.BlockSpec(shape, lambda bi, c: (0,) * len(shape))
    w2p = jnp.stack([_pad_lora(w2, d) for d in range(2)]).astype(BF16)
    a2p = jnp.stack([_pad_lora(a2, d) for d in range(2)]).astype(BF16)
    return pl.pallas_call(
        _rwkv_scan_kernel,
        grid=(b, nc),
        in_specs=[pl.BlockSpec((1, n, RW_IN), lambda bi, c: (bi, c, 0)),
                  pl.BlockSpec((1, n, RW_IN), lambda bi, c: (bi, bwd(c), 0)),
                  const((2, 1, cdim)), const((2, LANES, cdim)), const((2, 1, cdim)),
                  const((2, LANES, cdim)), const((1, cdim)), const((1, cdim)), const((cdim, cdim))],
        out_specs=[pl.BlockSpec((1, n, cdim), lambda bi, c: (bi, c, 0)),
                   pl.BlockSpec((1, n, cdim), lambda bi, c: (bi, bwd(c), 0))],
        out_shape=[jax.ShapeDtypeStruct((b, t, cdim), F32)] * 2,
        scratch_shapes=[pltpu.VMEM((2 * cdim // LANES, LANES, LANES), F32)],
        compiler_params=_cparams(("parallel", "arbitrary")),
    )(za, za, w0.reshape(2, 1, cdim), w2p, a0.reshape(2, 1, cdim), a2p, k_k.reshape(1, cdim),
      k_a.reshape(1, cdim), _block_diag_ones(cdim, RW_HEAD_DIM))


def _even_post_kernel(za_ref, yf_ref, yb_ref, zg_ref, of_ref, ob_ref, a0_ref, a2_ref, g2_ref,
                      ka_ref, rk_ref, gng_ref, gnb_ref, hgg_ref, bd64_ref, bd128_ref, wout_ref,
                      o_ref):
    cdim = RW_DIM
    z = za_ref[...]
    r, k, v = z[:, :cdim], z[:, cdim:2 * cdim], z[:, 2 * cdim:3 * cdim]
    a_sum = _rwkv_iclr(z, 0, a0_ref, a2_ref) + _rwkv_iclr(z, 1, a0_ref, a2_ref)
    k_sum = k * (2.0 + (a_sum - 2.0) * ka_ref[...])
    gate = _dot(jax.nn.sigmoid(z[:, RW_LORA_OFF + 2 * LANES:]), g2_ref[...])
    bd64 = bd64_ref[...]
    y = yf_ref[...] + yb_ref[...]
    mean = _seg_sum(y, bd64) * (1.0 / RW_HEAD_DIM)
    cen = y - mean
    var = _seg_sum(cen * cen, bd64) * (1.0 / RW_HEAD_DIM)
    y = cen * lax.rsqrt(var + RW_GN_EPS) * gng_ref[...] + gnb_ref[...]
    bonus = _seg_sum(r * k_sum * rk_ref[...], bd64) * v
    y_rw = (y + bonus) * gate

    o = of_ref[...] + ob_ref[...]
    ms = _seg_sum(o * o, bd128_ref[...]) * (1.0 / HG_KEY_DIM)
    g_hg = zg_ref[...]
    y_hg = o * lax.rsqrt(ms + RMS_EPS) * hgg_ref[...] * (g_hg * jax.nn.sigmoid(g_hg))

    y_all = jnp.concatenate([y_rw, y_hg], axis=1).astype(BF16)
    o_ref[...] = jnp.dot(y_all, wout_ref[...], preferred_element_type=F32)


def even_post(za, yf, yb, zb, of, ob, a0, a2, g2, k_a, r_k, gn_g, gn_b, hg_norm_g, w_out, tm=256):
    m = za.shape[0]
    cdim, dm = RW_DIM, w_out.shape[1]
    tm = _row_tile(m, tm)
    rows = lambda width, blk=0: pl.BlockSpec((tm, width), lambda i: (i, blk))
    const = lambda shape: pl.BlockSpec(shape, lambda i: (0,) * len(shape))
    a2p = jnp.stack([_pad_lora(a2, d) for d in range(2)]).astype(BF16)
    vec = lambda x: x.reshape(1, cdim)
    return pl.pallas_call(
        _even_post_kernel,
        grid=(m // tm,),
        in_specs=[rows(RW_IN), rows(cdim), rows(cdim), rows(HG_VDIM, (HG_IN - HG_VDIM) // HG_VDIM),
                  rows(HG_VDIM), rows(HG_VDIM),
                  const((2, 1, cdim)), const((2, LANES, cdim)), const((RW_GATE_LORA, cdim)),
                  const((1, cdim)), const((1, cdim)), const((1, cdim)), const((1, cdim)),
                  const((1, HG_VDIM)), const((cdim, cdim)), const((HG_VDIM, HG_VDIM)),
                  const((cdim + HG_VDIM, dm))],
        out_specs=rows(dm),
        out_shape=jax.ShapeDtypeStruct((m, dm), F32),
        compiler_params=_cparams(("parallel",)),
    )(za, yf, yb, zb, of, ob, a0.reshape(2, 1, cdim), a2p, g2.astype(BF16), vec(k_a), vec(r_k),
      vec(gn_g), vec(gn_b), hg_norm_g.reshape(1, HG_VDIM), _block_diag_ones(cdim, RW_HEAD_DIM),
      _block_diag_ones(HG_VDIM, HG_KEY_DIM), w_out.astype(BF16))


def _attn_kernel(q_ref, k_ref, v_ref, o_ref):
    s = _dot_nt(q_ref[0, 0], k_ref[0, 0]) * MLA_SCALE
    m = jnp.max(s, axis=-1, keepdims=True)
    p = jnp.exp(s - m)
    l = jnp.sum(p, axis=-1, keepdims=True)
    o_ref[0, 0] = jnp.dot(p.astype(BF16), v_ref[0, 0], preferred_element_type=F32) / l


def attention(q, k, v, tq=256):
    b, h, nq, dk = q.shape
    nk, dv = k.shape[2], v.shape[3]
    tq = min(tq, nq)
    return pl.pallas_call(
        _attn_kernel,
        grid=(b, h, nq // tq),
        in_specs=[pl.BlockSpec((1, 1, tq, dk), lambda bi, hi, qi: (bi, hi, qi, 0)),
                  pl.BlockSpec((1, 1, nk, dk), lambda bi, hi, qi: (bi, hi, 0, 0)),
                  pl.BlockSpec((1, 1, nk, dv), lambda bi, hi, qi: (bi, hi, 0, 0))],
        out_specs=pl.BlockSpec((1, 1, tq, dv), lambda bi, hi, qi: (bi, hi, qi, 0)),
        out_shape=jax.ShapeDtypeStruct((b, h, nq, dv), F32),
        compiler_params=_cparams(("parallel", "parallel", "parallel")),
    )(q, k, v)


def _moe_kernel(te_ref, nt_ref, x_ref, wgu_ref, bgu_ref, wd_ref, bd_ref, o_ref, wgu_bf, wd_bf):
    i = pl.program_id(0)
    live = i < nt_ref[0]
    new_expert = jnp.logical_or(i == 0, te_ref[i] != te_ref[jnp.maximum(i - 1, 0)])

    @pl.when(jnp.logical_and(live, new_expert))
    def _():
        wgu_bf[...] = wgu_ref[0].astype(BF16)
        wd_bf[...] = wd_ref[0].astype(BF16)

    @pl.when(live)
    def _():
        h = jnp.dot(x_ref[...], wgu_bf[...], preferred_element_type=F32) + bgu_ref[0]
        glu = jnp.minimum(h[:, :EXPERT_DIM], SWIGLU_LIMIT)
        lin = jnp.clip(h[:, EXPERT_DIM:], -SWIGLU_LIMIT, SWIGLU_LIMIT)
        act = glu * jax.nn.sigmoid(SWIGLU_ALPHA * glu) * (lin + 1.0)
        o_ref[...] = jnp.dot(act.astype(BF16), wd_bf[...], preferred_element_type=F32) + bd_ref[0]

    @pl.when(i >= nt_ref[0])
    def _():
        o_ref[...] = jnp.zeros_like(o_ref)


def moe_grouped(x_sorted, tile_expert, n_tiles_used, w_gu, b_gu, w_down, b_down):
    p, dm = x_sorted.shape
    tm = MOE_TILE
    e, _, f2 = w_gu.shape
    grid_spec = pltpu.PrefetchScalarGridSpec(
        num_scalar_prefetch=2,
        grid=(p // tm,),
        in_specs=[
            pl.BlockSpec((tm, dm), lambda i, te, nt: (i, 0)),
            pl.BlockSpec((1, dm, f2), lambda i, te, nt: (te[i], 0, 0)),
            pl.BlockSpec((1, 1, f2), lambda i, te, nt: (te[i], 0, 0)),
            pl.BlockSpec((1, f2 // 2, dm), lambda i, te, nt: (te[i], 0, 0)),
            pl.BlockSpec((1, 1, dm), lambda i, te, nt: (te[i], 0, 0)),
        ],
        out_specs=pl.BlockSpec((tm, dm), lambda i, te, nt: (i, 0)),
        scratch_shapes=[pltpu.VMEM((dm, f2), BF16), pltpu.VMEM((f2 // 2, dm), BF16)],
    )
    return pl.pallas_call(
        _moe_kernel,
        grid_spec=grid_spec,
        out_shape=jax.ShapeDtypeStruct((p, dm), F32),
        compiler_params=_cparams(("arbitrary",)),
    )(tile_expert, n_tiles_used, x_sorted, w_gu, b_gu.reshape(e, 1, f2), w_down,
      b_down.reshape(e, 1, dm))


def moe(u, w_router, b_router, w_gu, b_gu, w_down, b_down):
    n, dm = u.shape
    tm = MOE_TILE
    logits = router_logits(u, w_router, b_router)
    top_logit, top_idx = lax.top_k(logits, TOP_K)
    gates = jax.nn.softmax(top_logit, axis=-1)
    e_flat = top_idx.reshape(-1).astype(jnp.int32)
    npair = n * TOP_K
    rb = _row_tile(npair, RANK_BLOCK)
    onehot = (e_flat[:, None] == jnp.arange(N_EXPERTS, dtype=jnp.int32)[None, :])
    oh3 = onehot.astype(BF16).reshape(npair // rb, rb, N_EXPERTS)
    earlier = (jnp.arange(rb)[:, None] > jnp.arange(rb)[None, :]).astype(BF16)
    within = jnp.einsum('ij,bje->bie', earlier, oh3, preferred_element_type=F32)
    blk_tot = jnp.sum(oh3.astype(F32), axis=1)
    blk_off = jnp.cumsum(blk_tot, axis=0) - blk_tot
    counts = jnp.sum(blk_tot, axis=0).astype(jnp.int32)
    padded = ((counts + tm - 1) // tm) * tm
    ends_p = jnp.cumsum(padded)
    starts_p = ends_p - padded
    starts = jnp.cumsum(counts) - counts
    rank = within + blk_off[:, None, :] + starts_p.astype(F32)[None, None, :]
    pos = jnp.sum(oh3.astype(F32) * rank, axis=-1).astype(jnp.int32).reshape(npair)
    p_rows = npair + N_EXPERTS * tm
    n_tiles = p_rows // tm
    tile_expert = jnp.minimum(
        jnp.searchsorted(ends_p, jnp.arange(n_tiles, dtype=jnp.int32) * tm, side='right'),
        N_EXPERTS - 1).astype(jnp.int32)
    n_used = (ends_p[-1] // tm).astype(jnp.int32).reshape(1)
    _, sorted_tok = lax.sort_key_val(e_flat, jnp.arange(npair, dtype=jnp.int32) // TOP_K)
    shift = (starts - starts_p)[tile_expert]
    src = jnp.arange(p_rows, dtype=jnp.int32).reshape(n_tiles, tm) + shift[:, None]
    src_tok = sorted_tok[jnp.clip(src.reshape(-1), 0, npair - 1)]
    x_sorted = u.astype(BF16)[src_tok]
    y_sorted = moe_grouped(x_sorted, tile_expert, n_used, w_gu, b_gu, w_down, b_down)
    y = y_sorted[pos].reshape(n, TOP_K, dm)
    return jnp.sum(y * gates[:, :, None], axis=1)


def _layer_norm(x, g, b):
    mu = jnp.mean(x, -1, keepdims=True)
    var = jnp.mean(jnp.square(x - mu), -1, keepdims=True)
    return (x - mu) * lax.rsqrt(var + LN_EPS) * g + b


def _rms_normalize(x):
    return x * lax.rsqrt(jnp.mean(jnp.square(x), -1, keepdims=True) + RMS_EPS)


def _seg_shift(z, n_ctx):
    t = z.shape[1]
    idx = jnp.arange(t)[None, :, None]
    prev = jnp.pad(z[:, :-1], ((0, 0), (1, 0), (0, 0)))
    nxt = jnp.pad(z[:, 1:], ((0, 0), (0, 1), (0, 0)))
    prev = jnp.where(idx == n_ctx, 0.0, prev)
    nxt = jnp.where(idx == n_ctx - 1, 0.0, nxt)
    return 0.5 * (prev + nxt) - z


def _even_mixer(u, n_ctx, w_in, mu, w0, w2, a0, a2, g2, k_k, k_a, r_k, gn_g, gn_b, lower,
                hg_norm_g, w_out):
    b, t, dm = u.shape
    za, zb = matmul(u.reshape(b * t, dm), w_in, splits=(RW_IN, HG_IN))
    za = za.reshape(b, t, RW_IN)
    zb = zb.reshape(b, t, HG_IN)
    za = za + mu * _seg_shift(za, n_ctx)
    yf, yb = rwkv7_scan(za, w0, w2, a0, a2, k_k, k_a, n_ctx)
    of, ob = hgrn2_scan(zb, lower, n_ctx)
    flat = lambda x: x.reshape(b * t, x.shape[-1])
    y = even_post(flat(za), flat(yf), flat(yb), flat(zb), flat(of), flat(ob), a0, a2, g2, k_a,
                  r_k.reshape(-1), gn_g, gn_b, hg_norm_g, w_out)
    return y.reshape(b, t, dm)


def _rope_tables(rows):
    t = jnp.arange(rows * GRID_W)
    row = (t // GRID_W).astype(F32)
    col = (t % GRID_W).astype(F32)
    half = MLA_ROPE // 2
    inv_freq = ROPE_BASE ** (-jnp.arange(0, half, 2, dtype=F32) / half)
    ang_r = row[:, None] * inv_freq
    ang_c = col[:, None] * inv_freq
    ang = jnp.concatenate([ang_r, ang_r, ang_c, ang_c], axis=-1)
    return jnp.cos(ang), jnp.sin(ang)


def _rope(x, cos, sin):
    r1, r2, c1, c2 = jnp.split(x, 4, axis=-1)
    rot = jnp.concatenate([-r2, r1, -c2, c1], axis=-1)
    return x * cos + rot * sin


def _mla_mixer(u, n_ctx, w_in, q_norm_g, w_qb, kv_norm_g, w_kvb, w_out, cos, sin, need_ctx):
    b, t, dm = u.shape
    z = matmul(u.reshape(b * t, dm), w_in)
    zq = z[:, :MLA_Q_RANK]
    kva = z[:, MLA_Q_RANK:MLA_Q_RANK + MLA_KV_RANK]
    kpe = z[:, MLA_Q_RANK + MLA_KV_RANK:].reshape(b, t, MLA_ROPE)
    q = matmul(_rms_normalize(zq) * q_norm_g, w_qb).reshape(b, t, MLA_HEADS, MLA_QK)
    kv = matmul(_rms_normalize(kva) * kv_norm_g, w_kvb).reshape(b, t, MLA_HEADS, MLA_NOPE + MLA_V)
    q_rope = jnp.concatenate([q[:, :n_ctx, :, MLA_NOPE:],
                              _rope(q[:, n_ctx:, :, MLA_NOPE:], cos[:, None], sin[:, None])], axis=1)
    q = jnp.concatenate([q[..., :MLA_NOPE], q_rope], axis=-1)
    kpe = jnp.concatenate([kpe[:, :n_ctx], _rope(kpe[:, n_ctx:], cos, sin)], axis=1)
    k = jnp.concatenate([kv[..., :MLA_NOPE],
                         jnp.broadcast_to(kpe[:, :, None], (b, t, MLA_HEADS, MLA_ROPE))], axis=-1)
    v = kv[..., MLA_NOPE:]
    hm = lambda x: jnp.transpose(x, (0, 2, 1, 3)).astype(BF16)
    qh, kh, vh = hm(q), hm(k), hm(v)
    o_lat = attention(qh[:, :, n_ctx:], kh, vh)
    if need_ctx:
        o_ctx = attention(qh[:, :, :n_ctx], kh[:, :, :n_ctx], vh[:, :, :n_ctx])
        o = jnp.concatenate([o_ctx, o_lat], axis=2)
    else:
        o = jnp.concatenate([jnp.zeros((b, MLA_HEADS, n_ctx, MLA_V), F32), o_lat], axis=2)
    o = jnp.transpose(o, (0, 2, 1, 3)).reshape(b * t, MLA_HEADS * MLA_V)
    return matmul(o, w_out).reshape(b, t, dm)


def kernel(x, c, ctx, c_ctx, mod_w, mod_b, ln_g, ln_b, ev_w_in, rw_mu, rw_w0, rw_w2, rw_a0, rw_a2,
           rw_g2, rw_k_k, rw_k_a, rw_r_k, rw_gn_g, rw_gn_b, hg_lb, hg_norm_g, ev_w_out, od_w_in,
           mla_q_norm_g, mla_w_qb, mla_kv_norm_g, mla_w_kvb, od_w_out, moe_w_router, moe_b_router,
           moe_w_gu, moe_b_gu, moe_w_down, moe_b_down):
    b, n_lat, dm = x.shape
    n_ctx = ctx.shape[1]
    t = n_ctx + n_lat
    cos, sin = _rope_tables(n_lat // GRID_W)
    lb = jax.nn.softmax(hg_lb.astype(F32), axis=0)
    hg_lower = jnp.cumsum(lb, axis=0) - lb[0]
    c_act = c * jax.nn.sigmoid(c)
    cc_act = c_ctx * jax.nn.sigmoid(c_ctx)
    mod_in = jnp.concatenate([c_act, cc_act[None]], axis=0)
    pad = (-mod_in.shape[0]) % 8
    mod_in = jnp.pad(mod_in, ((0, pad), (0, 0)))
    is_ctx = (jnp.arange(t) < n_ctx)[None, :, None]
    h = jnp.concatenate([ctx, x], axis=1)
    for layer in range(DEPTH):
        need_ctx = layer < DEPTH - 1
        j = layer // 2
        mods = matmul(mod_in, mod_w[layer]) + mod_b[layer]
        m_lat = mods[:b].reshape(b, 1, N_MOD, dm)
        m_ctx = jnp.broadcast_to(mods[b].reshape(1, 1, N_MOD, dm), (b, 1, N_MOD, dm))
        m = [jnp.where(is_ctx, m_ctx[:, :, i], m_lat[:, :, i]) for i in range(N_MOD)]
        u = h * (1 + m[1]) + m[0]
        if layer % 2 == 0:
            y = _even_mixer(u, n_ctx, ev_w_in[j], rw_mu[j], rw_w0[j], rw_w2[j], rw_a0[j], rw_a2[j],
                            rw_g2[j], rw_k_k[j], rw_k_a[j], rw_r_k[j], rw_gn_g[j], rw_gn_b[j],
                            hg_lower[j], hg_norm_g[j], ev_w_out[j])
        else:
            y = _mla_mixer(u, n_ctx, od_w_in[j], mla_q_norm_g[j], mla_w_qb[j], mla_kv_norm_g[j],
                           mla_w_kvb[j], od_w_out[j], cos, sin, need_ctx)
        h = _layer_norm(DEEPNORM_ALPHA * h + m[2] * y, ln_g[layer, 0], ln_b[layer, 0])
        u = h * (1 + m[4]) + m[3]
        f = moe(u.reshape(b * t, dm), moe_w_router[layer], moe_b_router[layer], moe_w_gu[layer],
                moe_b_gu[layer], moe_w_down[layer], moe_b_down[layer]).reshape(b, t, dm)
        h = _layer_norm(DEEPNORM_ALPHA * h + m[5] * f, ln_g[layer, 1], ln_b[layer, 1])
    return h[:, n_ctx:]
```

```python
import functools

import jax
import jax.numpy as jnp
from jax import lax
from jax.experimental import pallas as pl
from jax.experimental.pallas import tpu as pltpu

F32 = jnp.float32
BF16 = jnp.bfloat16
HIGHEST = lax.Precision.HIGHEST

DEPTH = 4
GRID_W = 64
N_MOD = 6

RW_HEADS = 8
RW_HEAD_DIM = 64
RW_DIM = RW_HEADS * RW_HEAD_DIM
RW_DECAY_LORA = 64
RW_ICLR_LORA = 64
RW_GATE_LORA = 128
RW_GN_EPS = 64e-5
RW_IN = 3 * RW_DIM + 2 * RW_DECAY_LORA + 2 * RW_ICLR_LORA + RW_GATE_LORA
RW_LORA_OFF = 3 * RW_DIM

HG_HEADS = 4
HG_KEY_DIM = 128
HG_DIM = HG_HEADS * HG_KEY_DIM
HG_VDIM = HG_DIM
HG_IN = 3 * HG_DIM + 2 * HG_VDIM
HG_F_FLOOR = 1e-30

MLA_HEADS = 16
MLA_NOPE = 64
MLA_ROPE = 32
MLA_V = 64
MLA_Q_RANK = 256
MLA_KV_RANK = 128
MLA_QK = MLA_NOPE + MLA_ROPE
MLA_SCALE = MLA_QK ** -0.5
ROPE_BASE = 10000.0

N_EXPERTS = 32
TOP_K = 4
EXPERT_DIM = 1024
SWIGLU_LIMIT = 7.0
SWIGLU_ALPHA = 1.702

DEEPNORM_ALPHA = (2 * DEPTH) ** 0.25
LN_EPS = 1e-5
RMS_EPS = 1e-6

LANES = 128
SCAN_CHUNK = 64
HG_SUB = 16
MOE_TILE = 512
RANK_BLOCK = 256
VMEM_LIMIT = 56 * 1024 * 1024


def _cparams(sem):
    return pltpu.CompilerParams(dimension_semantics=sem, vmem_limit_bytes=VMEM_LIMIT)


def _row_tile(m, tm):
    if m <= tm:
        return m
    while m % tm:
        tm //= 2
    assert tm >= 8
    return tm


def _mm_kernel(x_ref, w_ref, *o_refs):
    acc = jnp.dot(x_ref[...].astype(BF16), w_ref[...], preferred_element_type=F32)
    off = 0
    for o_ref in o_refs:
        width = o_ref.shape[1]
        o_ref[...] = acc[:, off:off + width].astype(o_ref.dtype)
        off += width


def matmul(x, w, splits=None, out_dtype=F32, tm=512):
    m, k = x.shape
    n = w.shape[1]
    tm = _row_tile(m, tm)
    widths = (n,) if splits is None else splits
    assert sum(widths) == n
    outs = pl.pallas_call(
        _mm_kernel,
        grid=(m // tm,),
        in_specs=[pl.BlockSpec((tm, k), lambda i: (i, 0)),
                  pl.BlockSpec((k, n), lambda i: (0, 0))],
        out_specs=[pl.BlockSpec((tm, wd), lambda i: (i, 0)) for wd in widths],
        out_shape=[jax.ShapeDtypeStruct((m, wd), out_dtype) for wd in widths],
        compiler_params=_cparams(("parallel",)),
    )(x, w.astype(BF16))
    return outs[0] if splits is None else outs


def _router_kernel(x_ref, w_ref, b_ref, o_ref):
    o_ref[...] = jnp.dot(x_ref[...], w_ref[...], preferred_element_type=F32,
                         precision=HIGHEST) + b_ref[...]


def router_logits(x, w, b, tm=512):
    m, k = x.shape
    n = w.shape[1]
    tm = _row_tile(m, tm)
    return pl.pallas_call(
        _router_kernel,
        grid=(m // tm,),
        in_specs=[pl.BlockSpec((tm, k), lambda i: (i, 0)),
                  pl.BlockSpec((k, n), lambda i: (0, 0)),
                  pl.BlockSpec((1, n), lambda i: (0, 0))],
        out_specs=pl.BlockSpec((tm, n), lambda i: (i, 0)),
        out_shape=jax.ShapeDtypeStruct((m, n), F32),
        compiler_params=_cparams(("parallel",)),
    )(x, w, b.reshape(1, n))


def _bwd_chunk(c, n_ctx_chunks, n_chunks):
    return jnp.where(c < n_ctx_chunks, n_ctx_chunks - 1 - c, n_chunks - 1 - (c - n_ctx_chunks))


def _dot(a, b):
    return jnp.dot(a.astype(BF16), b.astype(BF16), preferred_element_type=F32)


def _dot_nt(a, b):
    return lax.dot_general(a.astype(BF16), b.astype(BF16), (((1,), (1,)), ((), ())),
                           preferred_element_type=F32)


def _dot_tn(a, b):
    return lax.dot_general(a.astype(BF16), b.astype(BF16), (((0,), (0,)), ((), ())),
                           preferred_element_type=F32)


def _dot_f32(a, b):
    return jnp.dot(a, b, preferred_element_type=F32, precision=HIGHEST)


def _seg_sum(x, ones_bd):
    hi = x.astype(BF16)
    lo = (x - hi.astype(F32)).astype(BF16)
    return (jnp.dot(hi, ones_bd, preferred_element_type=F32)
            + jnp.dot(lo, ones_bd, preferred_element_type=F32))


def _block_diag_ones(width, block):
    i = jnp.arange(width) // block
    return (i[:, None] == i[None, :]).astype(BF16)


def _hgrn_kernel(zf_ref, zb_ref, lo_ref, bd_ref, of_ref, ob_ref, st_ref):
    c = pl.program_id(1)
    n, sub = SCAN_CHUNK, HG_SUB
    nb = n // sub

    @pl.when(c == 0)
    def _():
        st_ref[...] = jnp.zeros_like(st_ref)

    row = lax.broadcasted_iota(jnp.int32, (n, n), 0)
    col = lax.broadcasted_iota(jnp.int32, (n, n), 1)
    colb = lax.broadcasted_iota(jnp.int32, (sub, LANES), 1)
    rown = lax.broadcasted_iota(jnp.int32, (sub, n), 0)
    coln = lax.broadcasted_iota(jnp.int32, (sub, n), 1)
    bd2 = bd_ref[...]

    dirs = []
    for d, z_ref in ((0, zf_ref), (1, zb_ref)):
        z = z_ref[0]
        lower = lo_ref[d]
        qraw = z[:, :HG_DIM]
        sig = jax.nn.sigmoid(z[:, HG_DIM * (1 + d):HG_DIM * (2 + d)])
        v = z[:, 3 * HG_DIM:3 * HG_DIM + HG_VDIM]
        f = lower + (1.0 - lower) * sig
        g = jnp.log(jnp.maximum(f, HG_F_FLOOR))
        kin = (1.0 - lower) * (1.0 - sig)
        q = qraw * jax.nn.sigmoid(qraw)
        lag = (row - col) if d == 0 else (col - row)
        cum = _dot_f32((lag >= 0).astype(F32), g)
        cum_ex = cum - g
        tot = jnp.sum(g, axis=0, keepdims=True)
        blocks = []
        for j in range(nb):
            lo, hi = j * sub, (j + 1) * sub
            q_j, cum_j = q[lo:hi], cum[lo:hi]
            ref_j = cum_ex[lo:lo + 1] if d == 0 else cum_ex[hi - 1:hi]
            qt = q_j * jnp.exp(cum_j - ref_j)
            kt = kin * jnp.exp(jnp.minimum(ref_j - cum, 0.0))
            pair = jnp.concatenate(
                [q_j * kin[s:s + 1] * jnp.exp(jnp.minimum(cum_j - cum[s:s + 1], 0.0))
                 for s in range(lo, hi)], axis=0)
            blocks.append((qt, kt, pair))
        dirs.append(dict(d=d, v=v, qe=q * jnp.exp(cum), kdec=kin * jnp.exp(tot - cum),
                         etot=jnp.exp(tot), blocks=blocks))

    for dd, o_ref in zip(dirs, (of_ref, ob_ref)):
        d = dd['d']
        for j, (qt, kt, pair) in enumerate(dd['blocks']):
            lo, hi = j * sub, (j + 1) * sub
            sums = [_dot(pair[:, 2 * LANES * hp:2 * LANES * (hp + 1)], bd2) for hp in range(HG_HEADS // 2)]
            for h in range(HG_HEADS):
                hs = slice(LANES * h, LANES * (h + 1))
                rs = sums[h // 2][:, LANES * (h % 2):LANES * (h % 2 + 1)]
                diag = jnp.zeros((sub, LANES), F32)
                for s in range(sub):
                    diag = jnp.where(colb == lo + s, rs[s * sub:(s + 1) * sub], diag)
                off = _dot_nt(qt[:, hs], kt[:, hs])
                if d == 0:
                    earlier, ordered = coln < lo, rown + lo >= coln
                else:
                    earlier, ordered = coln >= hi, rown + lo <= coln
                sc = jnp.where(earlier, off, jnp.where(ordered, diag[:, :n], 0.0))
                dd.setdefault(('out', h), []).append(_dot(sc, dd['v'][:, hs]))

    for dd, o_ref in zip(dirs, (of_ref, ob_ref)):
        d = dd['d']
        for h in range(HG_HEADS):
            hs = slice(LANES * h, LANES * (h + 1))
            st = st_ref[d * HG_HEADS + h]
            out = jnp.concatenate(dd[('out', h)], axis=0) + _dot_nt(dd['qe'][:, hs], st)
            o_ref[0, :, hs] = out
            st_ref[d * HG_HEADS + h] = (st * dd['etot'][:, hs]
                                        + _dot_tn(dd['v'][:, hs], dd['kdec'][:, hs]))


def hgrn2_scan(zb, lower, n_ctx):
    b, t, _ = zb.shape
    n = SCAN_CHUNK
    nc, ncc = t // n, n_ctx // n
    bwd = functools.partial(_bwd_chunk, n_ctx_chunks=ncc, n_chunks=nc)
    const = lambda shape: pl.BlockSpec(shape, lambda bi, c: (0,) * len(shape))
    return pl.pallas_call(
        _hgrn_kernel,
        grid=(b, nc),
        in_specs=[pl.BlockSpec((1, n, HG_IN), lambda bi, c: (bi, c, 0)),
                  pl.BlockSpec((1, n, HG_IN), lambda bi, c: (bi, bwd(c), 0)),
                  const((2, 1, HG_DIM)), const((2 * LANES, 2 * LANES))],
        out_specs=[pl.BlockSpec((1, n, HG_VDIM), lambda bi, c: (bi, c, 0)),
                   pl.BlockSpec((1, n, HG_VDIM), lambda bi, c: (bi, bwd(c), 0))],
        out_shape=[jax.ShapeDtypeStruct((b, t, HG_VDIM), F32)] * 2,
        scratch_shapes=[pltpu.VMEM((2 * HG_HEADS, HG_KEY_DIM, HG_KEY_DIM), F32)],
        compiler_params=_cparams(("parallel", "arbitrary")),
    )(zb, zb, lower.reshape(2, 1, HG_DIM), _block_diag_ones(2 * LANES, LANES))


def _rwkv_iclr(z, d, a0_ref, a2_ref):
    ad = z[:, RW_LORA_OFF + LANES:RW_LORA_OFF + 2 * LANES]
    return jax.nn.sigmoid(a0_ref[d] + _dot(ad, a2_ref[d]))


def _rwkv_scan_kernel(zf_ref, zb_ref, w0_ref, w2_ref, a0_ref, a2_ref, kk_ref, ka_ref, bd_ref,
                      yf_ref, yb_ref, st_ref):
    c = pl.program_id(1)
    n, cdim = SCAN_CHUNK, RW_DIM
    npair = cdim // LANES

    @pl.when(c == 0)
    def _():
        st_ref[...] = jnp.zeros_like(st_ref)

    row = lax.broadcasted_iota(jnp.int32, (n, n), 0)
    col = lax.broadcasted_iota(jnp.int32, (n, n), 1)
    row2 = lax.broadcasted_iota(jnp.int32, (n, 2 * n), 0)
    col2 = lax.broadcasted_iota(jnp.int32, (n, 2 * n), 1) % n
    lane = lax.broadcasted_iota(jnp.int32, (1, LANES), 1)
    half = [lane < RW_HEAD_DIM, lane >= RW_HEAD_DIM]
    first2 = lax.broadcasted_iota(jnp.int32, (1, 2 * LANES), 1) % LANES < RW_HEAD_DIM
    r2 = lax.broadcasted_iota(jnp.int32, (LANES, LANES), 0) < RW_HEAD_DIM
    c2 = lax.broadcasted_iota(jnp.int32, (LANES, LANES), 1) < RW_HEAD_DIM
    same_head = r2 == c2

    chains = []
    for d, z_ref in ((0, zf_ref), (1, zb_ref)):
        z = z_ref[0]
        r, k, v = z[:, :cdim], z[:, cdim:2 * cdim], z[:, 2 * cdim:3 * cdim]
        wd = z[:, RW_LORA_OFF:RW_LORA_OFF + LANES]
        w_pre = w0_ref[d] + _dot(jnp.tanh(wd), w2_ref[d])
        lw = -jnp.exp(-0.5) * jax.nn.sigmoid(w_pre)
        a = _rwkv_iclr(z, d, a0_ref, a2_ref)
        kk = k * kk_ref[...]
        kk = kk * lax.rsqrt(jnp.maximum(_seg_sum(kk * kk, bd_ref[...]), 1e-24))
        kd = k * (1.0 + (a - 1.0) * ka_ref[...])
        bb = kk * a
        lag = (row - col) if d == 0 else (col - row)
        lag2 = (row2 - col2) if d == 0 else (col2 - row2)
        incl2, strict2 = lag2 >= 0, lag2 > 0
        cum = _dot_f32((lag >= 0).astype(F32), lw)
        tot = jnp.sum(lw, axis=0, keepdims=True)
        g_inv = jnp.exp(-cum)
        g_tail = jnp.exp(tot - cum)
        rt = r * jnp.exp(cum)
        at = -kk * jnp.exp(cum - lw)
        etot = jnp.exp(tot)
        left = jnp.concatenate([at, rt], axis=0)
        right = jnp.concatenate([bb * g_inv, kd * g_inv], axis=0)
        tail = jnp.concatenate([bb * g_tail, kd * g_tail], axis=0)
        for p in range(npair):
            ps = slice(LANES * p, LANES * (p + 1))
            chains.append(dict(d=d, p=p, ps=ps, incl2=incl2, strict2=strict2, left=left[:, ps],
                               right=right[:, ps], tail=tail[:, ps], at=at[:, ps], rt=rt[:, ps],
                               v=v[:, ps], etot=etot[:, ps]))

    for ch in chains:
        ch['top'], ch['bot'] = [], []
        for hh in range(2):
            prod = _dot_nt(ch['left'], jnp.where(half[hh], ch['right'], 0.0))
            ch['top'].append(jnp.where(ch['strict2'], prod[:n], 0.0))
            ch['bot'].append(jnp.where(ch['incl2'], prod[n:], 0.0))
    for ch in chains:
        av = [_dot(ch['top'][hh][:, n:], ch['v']) for hh in range(2)]
        x0 = jnp.concatenate([ch['at'], jnp.where(half[0], av[0], av[1])], axis=1)
        ch['x'] = [x0, x0]
        ch['a'] = [ch['top'][hh][:, :n] for hh in range(2)]
    steps = n.bit_length() - 1
    for i in range(steps):
        for ch in chains:
            for hh in range(2):
                ch['x'][hh] = ch['x'][hh] + _dot(ch['a'][hh], ch['x'][hh])
                if i + 1 < steps:
                    ch['a'][hh] = _dot(ch['a'][hh], ch['a'][hh])
    for ch in chains:
        x = jnp.where(first2, ch['x'][0], ch['x'][1])
        st = st_ref[ch['d'] * npair + ch['p']]
        ws = _dot_nt(jnp.concatenate([x[:, :LANES], ch['rt']], axis=0), st)
        u = ws[:n] + x[:, LANES:]
        uv = jnp.concatenate([u, ch['v']], axis=0)
        y = ws[n:] + jnp.where(half[0], _dot(ch['bot'][0], uv), _dot(ch['bot'][1], uv))
        y_ref = yf_ref if ch['d'] == 0 else yb_ref
        y_ref[0, :, ch['ps']] = y
        delta = _dot_tn(uv, ch['tail'])
        st_ref[ch['d'] * npair + ch['p']] = st * ch['etot'] + jnp.where(same_head, delta, 0.0)


def _pad_lora(w, d):
    return jnp.concatenate([w[d] if i == d else jnp.zeros_like(w[i]) for i in range(2)], axis=0)


def rwkv7_scan(za, w0, w2, a0, a2, k_k, k_a, n_ctx):
    b, t, _ = za.shape
    n, cdim = SCAN_CHUNK, RW_DIM
    nc, ncc = t // n, n_ctx // n
    bwd = functools.partial(_bwd_chunk, n_ctx_chunks=ncc, n_chunks=nc)
    const = lambda shape: pl.BlockSpec(shape, lambda bi, c: (0,) * len(shape))
    w2p = jnp.stack([_pad_lora(w2, d) for d in range(2)]).astype(BF16)
    a2p = jnp.stack([_pad_lora(a2, d) for d in range(2)]).astype(BF16)
    return pl.pallas_call(
        _rwkv_scan_kernel,
        grid=(b, nc),
        in_specs=[pl.BlockSpec((1, n, RW_IN), lambda bi, c: (bi, c, 0)),
                  pl.BlockSpec((1, n, RW_IN), lambda bi, c: (bi, bwd(c), 0)),
                  const((2, 1, cdim)), const((2, LANES, cdim)), const((2, 1, cdim)),
                  const((2, LANES, cdim)), const((1, cdim)), const((1, cdim)), const((cdim, cdim))],
        out_specs=[pl.BlockSpec((1, n, cdim), lambda bi, c: (bi, c, 0)),
                   pl.BlockSpec((1, n, cdim), lambda bi, c: (bi, bwd(c), 0))],
        out_shape=[jax.ShapeDtypeStruct((b, t, cdim), F32)] * 2,
        scratch_shapes=[pltpu.VMEM((2 * cdim // LANES, LANES, LANES), F32)],
        compiler_params=_cparams(("parallel", "arbitrary")),
    )(za, za, w0.reshape(2, 1, cdim), w2p, a0.reshape(2, 1, cdim), a2p, k_k.reshape(1, cdim),
      k_a.reshape(1, cdim), _block_diag_ones(cdim, RW_HEAD_DIM))


def _even_post_kernel(za_ref, yf_ref, yb_ref, zg_ref, of_ref, ob_ref, a0_ref, a2_ref, g2_ref,
                      ka_ref, rk_ref, gng_ref, gnb_ref, hgg_ref, bd64_ref, bd128_ref, wout_ref,
                      o_ref):
    cdim = RW_DIM
    z = za_ref[...]
    r, k, v = z[:, :cdim], z[:, cdim:2 * cdim], z[:, 2 * cdim:3 * cdim]
    a_sum = _rwkv_iclr(z, 0, a0_ref, a2_ref) + _rwkv_iclr(z, 1, a0_ref, a2_ref)
    k_sum = k * (2.0 + (a_sum - 2.0) * ka_ref[...])
    gate = _dot(jax.nn.sigmoid(z[:, RW_LORA_OFF + 2 * LANES:]), g2_ref[...])
    bd64 = bd64_ref[...]
    y = yf_ref[...] + yb_ref[...]
    mean = _seg_sum(y, bd64) * (1.0 / RW_HEAD_DIM)
    cen = y - mean
    var = _seg_sum(cen * cen, bd64) * (1.0 / RW_HEAD_DIM)
    y = cen * lax.rsqrt(var + RW_GN_EPS) * gng_ref[...] + gnb_ref[...]
    bonus = _seg_sum(r * k_sum * rk_ref[...], bd64) * v
    y_rw = (y + bonus) * gate

    o = of_ref[...] + ob_ref[...]
    ms = _seg_sum(o * o, bd128_ref[...]) * (1.0 / HG_KEY_DIM)
    g_hg = zg_ref[...]
    y_hg = o * lax.rsqrt(ms + RMS_EPS) * hgg_ref[...] * (g_hg * jax.nn.sigmoid(g_hg))

    y_all = jnp.concatenate([y_rw, y_hg], axis=1).astype(BF16)
    o_ref[...] = jnp.dot(y_all, wout_ref[...], preferred_element_type=F32)


def even_post(za, yf, yb, zb, of, ob, a0, a2, g2, k_a, r_k, gn_g, gn_b, hg_norm_g, w_out, tm=256):
    m = za.shape[0]
    cdim, dm = RW_DIM, w_out.shape[1]
    tm = _row_tile(m, tm)
    rows = lambda width, blk=0: pl.BlockSpec((tm, width), lambda i: (i, blk))
    const = lambda shape: pl.BlockSpec(shape, lambda i: (0,) * len(shape))
    a2p = jnp.stack([_pad_lora(a2, d) for d in range(2)]).astype(BF16)
    vec = lambda x: x.reshape(1, cdim)
    return pl.pallas_call(
        _even_post_kernel,
        grid=(m // tm,),
        in_specs=[rows(RW_IN), rows(cdim), rows(cdim), rows(HG_VDIM, (HG_IN - HG_VDIM) // HG_VDIM),
                  rows(HG_VDIM), rows(HG_VDIM),
                  const((2, 1, cdim)), const((2, LANES, cdim)), const((RW_GATE_LORA, cdim)),
                  const((1, cdim)), const((1, cdim)), const((1, cdim)), const((1, cdim)),
                  const((1, HG_VDIM)), const((cdim, cdim)), const((HG_VDIM, HG_VDIM)),
                  const((cdim + HG_VDIM, dm))],
        out_specs=rows(dm),
        out_shape=jax.ShapeDtypeStruct((m, dm), F32),
        compiler_params=_cparams(("parallel",)),
    )(za, yf, yb, zb, of, ob, a0.reshape(2, 1, cdim), a2p, g2.astype(BF16), vec(k_a), vec(r_k),
      vec(gn_g), vec(gn_b), hg_norm_g.reshape(1, HG_VDIM), _block_diag_ones(cdim, RW_HEAD_DIM),
      _block_diag_ones(HG_VDIM, HG_KEY_DIM), w_out.astype(BF16))


MLA_SLOT = 2 * LANES
MLA_GROUP = 4


def _rot_half(w):
    r1, r2, c1, c2 = jnp.split(w, 4, axis=-1)
    return jnp.concatenate([-r2, r1, -c2, c1], axis=-1)


def _mla_kv_kernel(z_ref, cs_ref, g_ref, w_ref, o_ref):
    z = z_ref[...]
    kva = z[:, :MLA_KV_RANK]
    kva = kva * lax.rsqrt(jnp.mean(kva * kva, axis=-1, keepdims=True) + RMS_EPS) * g_ref[...]
    cs = cs_ref[...]
    kpe = (z[:, MLA_KV_RANK:MLA_KV_RANK + MLA_ROPE] * cs[:, :MLA_ROPE]
           + z[:, MLA_KV_RANK + MLA_ROPE:] * cs[:, MLA_ROPE:])
    x = jnp.concatenate([kva, kpe], axis=1).astype(BF16)
    o_ref[...] = jnp.dot(x, w_ref[...], preferred_element_type=F32).astype(o_ref.dtype)


def mla_kv(z_kv, cs, kv_norm_g, w_kv_slots, rows_per_seq, tm=256):
    m, width = z_kv.shape
    tm = _row_tile(rows_per_seq, tm)
    per_seq = rows_per_seq // tm
    n_out = w_kv_slots.shape[1]
    return pl.pallas_call(
        _mla_kv_kernel,
        grid=(m // tm,),
        in_specs=[pl.BlockSpec((tm, width), lambda i: (i, 0)),
                  pl.BlockSpec((tm, 2 * MLA_ROPE), lambda i: (i % per_seq, 0)),
                  pl.BlockSpec((1, MLA_KV_RANK), lambda i: (0, 0)),
                  pl.BlockSpec(w_kv_slots.shape, lambda i: (0, 0))],
        out_specs=pl.BlockSpec((tm, n_out), lambda i: (i, 0)),
        out_shape=jax.ShapeDtypeStruct((m, n_out), BF16),
        compiler_params=_cparams(("parallel",)),
    )(z_kv, cs, kv_norm_g.reshape(1, MLA_KV_RANK), w_kv_slots)


def _mla_attn_kernel(zq_ref, kv_ref, cos_ref, sin_ref, g_ref, wq_ref, wr_ref, o_ref):
    zq = zq_ref[0]
    zn = (zq * lax.rsqrt(jnp.mean(zq * zq, axis=-1, keepdims=True) + RMS_EPS) * g_ref[...]).astype(BF16)
    a = jnp.dot(zn, wq_ref[...], preferred_element_type=F32)
    ar = jnp.dot(zn, wr_ref[...], preferred_element_type=F32)
    cos, sin = cos_ref[...], sin_ref[...]
    low = lax.broadcasted_iota(jnp.int32, (1, LANES), 1) < MLA_V
    blocks = []
    for pair in range(MLA_GROUP // 2):
        res = []
        for j in range(2):
            h = 2 * pair + j
            q_nope = a[:, MLA_SLOT * h:MLA_SLOT * h + LANES]
            q_rope = (a[:, MLA_SLOT * h + LANES:MLA_SLOT * (h + 1)] * cos
                      + ar[:, LANES * h:LANES * (h + 1)] * sin)
            qf = (jnp.concatenate([q_nope, q_rope], axis=1) * MLA_SCALE).astype(BF16)
            s = lax.dot_general(qf, kv_ref[0, :, MLA_SLOT * h:MLA_SLOT * (h + 1)],
                                (((1,), (1,)), ((), ())), preferred_element_type=F32)
            p = jnp.exp(s - jnp.max(s, axis=-1, keepdims=True))
            l = jnp.sum(p, axis=-1, keepdims=True)
            res.append(jnp.dot(p.astype(BF16), kv_ref[0, :, MLA_SLOT * h:MLA_SLOT * h + LANES],
                               preferred_element_type=F32) / l)
        blocks.append(jnp.where(low, res[1], res[0]))
    o_ref[0] = jnp.concatenate(blocks, axis=1).astype(o_ref.dtype)


def mla_attention(zq, kv_slots, n_keys, cos, sin, q_norm_g, wq_slots, wr_slots, tq=512):
    b, nq, _ = zq.shape
    tq = _row_tile(nq, tq)
    gw = MLA_GROUP * MLA_SLOT
    return pl.pallas_call(
        _mla_attn_kernel,
        grid=(b, MLA_HEADS // MLA_GROUP, nq // tq),
        in_specs=[pl.BlockSpec((1, tq, MLA_Q_RANK), lambda bi, g, qi: (bi, qi, 0)),
                  pl.BlockSpec((1, n_keys, gw), lambda bi, g, qi: (bi, 0, g)),
                  pl.BlockSpec((tq, LANES), lambda bi, g, qi: (qi, 0)),
                  pl.BlockSpec((tq, LANES), lambda bi, g, qi: (qi, 0)),
                  pl.BlockSpec((1, MLA_Q_RANK), lambda bi, g, qi: (0, 0)),
                  pl.BlockSpec((MLA_Q_RANK, gw), lambda bi, g, qi: (0, g)),
                  pl.BlockSpec((MLA_Q_RANK, gw // 2), lambda bi, g, qi: (0, g))],
        out_specs=pl.BlockSpec((1, tq, MLA_GROUP * MLA_V), lambda bi, g, qi: (bi, qi, g)),
        out_shape=jax.ShapeDtypeStruct((b, nq, MLA_HEADS * MLA_V), BF16),
        compiler_params=_cparams(("parallel", "parallel", "parallel")),
    )(zq, kv_slots, cos, sin, q_norm_g.reshape(1, MLA_Q_RANK), wq_slots, wr_slots)


def _moe_kernel(te_ref, nt_ref, x_ref, wgu_ref, bgu_ref, wd_ref, bd_ref, o_ref, wgu_bf, wd_bf):
    i = pl.program_id(0)
    live = i < nt_ref[0]
    new_expert = jnp.logical_or(i == 0, te_ref[i] != te_ref[jnp.maximum(i - 1, 0)])

    @pl.when(jnp.logical_and(live, new_expert))
    def _():
        wgu_bf[...] = wgu_ref[0, 0].astype(BF16)
        wd_bf[...] = wd_ref[0, 0].astype(BF16)

    @pl.when(live)
    def _():
        h = jnp.dot(x_ref[...], wgu_bf[...], preferred_element_type=F32) + bgu_ref[0, 0]
        glu = jnp.minimum(h[:, :EXPERT_DIM], SWIGLU_LIMIT)
        lin = jnp.clip(h[:, EXPERT_DIM:], -SWIGLU_LIMIT, SWIGLU_LIMIT)
        act = glu * jax.nn.sigmoid(SWIGLU_ALPHA * glu) * (lin + 1.0)
        y = jnp.dot(act.astype(BF16), wd_bf[...], preferred_element_type=F32) + bd_ref[0, 0]
        o_ref[...] = y.astype(o_ref.dtype)

    @pl.when(i >= nt_ref[0])
    def _():
        o_ref[...] = jnp.zeros_like(o_ref)


def moe_grouped(x_sorted, tile_expert, n_tiles_used, layer, w_gu, b_gu, w_down, b_down):
    p, dm = x_sorted.shape
    tm = MOE_TILE
    nl, e, _, f2 = w_gu.shape
    grid_spec = pltpu.PrefetchScalarGridSpec(
        num_scalar_prefetch=2,
        grid=(p // tm,),
        in_specs=[
            pl.BlockSpec((tm, dm), lambda i, te, nt: (i, 0)),
            pl.BlockSpec((1, 1, dm, f2), lambda i, te, nt: (layer, te[i], 0, 0)),
            pl.BlockSpec((1, 1, 1, f2), lambda i, te, nt: (layer, te[i], 0, 0)),
            pl.BlockSpec((1, 1, f2 // 2, dm), lambda i, te, nt: (layer, te[i], 0, 0)),
            pl.BlockSpec((1, 1, 1, dm), lambda i, te, nt: (layer, te[i], 0, 0)),
        ],
        out_specs=pl.BlockSpec((tm, dm), lambda i, te, nt: (i, 0)),
        scratch_shapes=[pltpu.VMEM((dm, f2), BF16), pltpu.VMEM((f2 // 2, dm), BF16)],
    )
    return pl.pallas_call(
        _moe_kernel,
        grid_spec=grid_spec,
        out_shape=jax.ShapeDtypeStruct((p, dm), BF16),
        compiler_params=_cparams(("arbitrary",)),
    )(tile_expert, n_tiles_used, x_sorted, w_gu, b_gu.reshape(nl, e, 1, f2), w_down,
      b_down.reshape(nl, e, 1, dm))


def moe(u, u16, layer, w_router, b_router, w_gu, b_gu, w_down, b_down):
    n, dm = u.shape
    tm = MOE_TILE
    logits = router_logits(u, w_router, b_router)
    top_logit, top_idx = lax.top_k(logits, TOP_K)
    gates = jax.nn.softmax(top_logit, axis=-1)
    e_flat = top_idx.T.reshape(-1).astype(jnp.int32)
    npair = n * TOP_K
    rb = _row_tile(npair, RANK_BLOCK)
    onehot = (e_flat[:, None] == jnp.arange(N_EXPERTS, dtype=jnp.int32)[None, :])
    oh3 = onehot.astype(BF16).reshape(npair // rb, rb, N_EXPERTS)
    earlier = (jnp.arange(rb)[:, None] > jnp.arange(rb)[None, :]).astype(BF16)
    within = jnp.einsum('ij,bje->bie', earlier, oh3, preferred_element_type=F32)
    blk_tot = jnp.sum(oh3.astype(F32), axis=1)
    blk_off = jnp.cumsum(blk_tot, axis=0) - blk_tot
    counts = jnp.sum(blk_tot, axis=0).astype(jnp.int32)
    padded = ((counts + tm - 1) // tm) * tm
    ends_p = jnp.cumsum(padded)
    starts_p = ends_p - padded
    starts = jnp.cumsum(counts) - counts
    rank = within + blk_off[:, None, :] + starts_p.astype(F32)[None, None, :]
    pos = jnp.sum(oh3.astype(F32) * rank, axis=-1).astype(jnp.int32).reshape(npair)
    p_rows = npair + N_EXPERTS * tm
    n_tiles = p_rows // tm
    tile_expert = jnp.minimum(
        jnp.searchsorted(ends_p, jnp.arange(n_tiles, dtype=jnp.int32) * tm, side='right'),
        N_EXPERTS - 1).astype(jnp.int32)
    n_used = (ends_p[-1] // tm).astype(jnp.int32).reshape(1)
    _, sorted_tok = lax.sort_key_val(e_flat, jnp.arange(npair, dtype=jnp.int32) % n)
    shift = (starts - starts_p)[tile_expert]
    src = jnp.arange(p_rows, dtype=jnp.int32).reshape(n_tiles, tm) + shift[:, None]
    src_tok = sorted_tok[jnp.clip(src.reshape(-1), 0, npair - 1)]
    x_sorted = u16[src_tok]
    y_sorted = moe_grouped(x_sorted, tile_expert, n_used, layer, w_gu, b_gu, w_down, b_down)
    y = y_sorted[pos].reshape(TOP_K, n, dm).astype(F32)
    return jnp.sum(y * gates.T[:, :, None], axis=0)


def _norm_mod_kernel(*refs, dm, gate, mod, has_norm, outs):
    refs = list(refs)
    h = refs.pop(0)[...]
    if has_norm:
        y = refs.pop(0)[...]
        mg = refs.pop(0)
        lng, lnb = refs.pop(0)[...], refs.pop(0)[...]
        x = DEEPNORM_ALPHA * h + mg[0, :, gate * dm:(gate + 1) * dm] * y
        mu = jnp.mean(x, axis=-1, keepdims=True)
        cen = x - mu
        var = jnp.mean(cen * cen, axis=-1, keepdims=True)
        h = cen * lax.rsqrt(var + LN_EPS) * lng + lnb
    if mod is not None:
        mm = refs.pop(0)
        shift, scale = mod
        u = h * (1.0 + mm[0, :, scale * dm:(scale + 1) * dm]) + mm[0, :, shift * dm:(shift + 1) * dm]
    for kind, o_ref in zip(outs, refs):
        o_ref[...] = (h if kind == 'h' else u).astype(o_ref.dtype)


def norm_mod(h, y, mods, rows_per_seq, n_ctx, n_batch, gate=None, ln=None, mod=None, outs=('h',),
             tm=256):
    m, dm = h.shape
    tm = _row_tile(n_ctx, tm)
    per_seq, ctx_tiles = rows_per_seq // tm, n_ctx // tm
    rows = mods.shape[0] // DEPTH

    def mod_row(layer):
        return lambda i: (layer * rows + jnp.where(i % per_seq < ctx_tiles, n_batch, i // per_seq), 0, 0)

    tile = pl.BlockSpec((tm, dm), lambda i: (i, 0))
    vec = pl.BlockSpec((1, dm), lambda i: (0, 0))
    args, specs = [h], [tile]
    if gate is not None:
        args += [y, mods, ln[0].reshape(1, dm), ln[1].reshape(1, dm)]
        specs += [tile, pl.BlockSpec((1, 1, N_MOD * dm), mod_row(gate[0])), vec, vec]
    if mod is not None:
        args.append(mods)
        specs.append(pl.BlockSpec((1, 1, N_MOD * dm), mod_row(mod[0])))
    dtypes = {'h': F32, 'u32': F32, 'u16': BF16}
    res = pl.pallas_call(
        functools.partial(_norm_mod_kernel, dm=dm, gate=None if gate is None else gate[1],
                          mod=None if mod is None else mod[1:], has_norm=gate is not None, outs=outs),
        grid=(m // tm,),
        in_specs=specs,
        out_specs=[tile] * len(outs),
        out_shape=[jax.ShapeDtypeStruct((m, dm), dtypes[k]) for k in outs],
        compiler_params=_cparams(("parallel",)),
    )(*args)
    return res[0] if len(outs) == 1 else res


SUBLANES = 8


def _shift_kernel(z_ref, prev_ref, next_ref, mu_ref, o_ref, *, per_seq, ctx_tiles):
    j = pl.program_id(0) % per_seq
    z = z_ref[...]
    tm = z.shape[0]
    starts_part = jnp.logical_or(j == 0, j == ctx_tiles)
    ends_part = jnp.logical_or(j == ctx_tiles - 1, j == per_seq - 1)
    before = prev_ref[SUBLANES - 1:SUBLANES, :] * jnp.where(starts_part, 0.0, 1.0)
    after = next_ref[0:1, :] * jnp.where(ends_part, 0.0, 1.0)
    row = lax.broadcasted_iota(jnp.int32, (tm, 1), 0)
    prev = jnp.where(row == 0, before, pltpu.roll(z, 1, axis=0))
    nxt = jnp.where(row == tm - 1, after, pltpu.roll(z, tm - 1, axis=0))
    o_ref[...] = z + mu_ref[...] * (0.5 * (prev + nxt) - z)


def token_shift(z, mu, rows_per_seq, n_ctx, tm=256):
    m, width = z.shape
    tm = _row_tile(n_ctx, tm)
    per_tile = tm // SUBLANES
    last = m // SUBLANES - 1
    return pl.pallas_call(
        functools.partial(_shift_kernel, per_seq=rows_per_seq // tm, ctx_tiles=n_ctx // tm),
        grid=(m // tm,),
        in_specs=[pl.BlockSpec((tm, width), lambda i: (i, 0)),
                  pl.BlockSpec((SUBLANES, width), lambda i: (jnp.maximum(i * per_tile - 1, 0), 0)),
                  pl.BlockSpec((SUBLANES, width), lambda i: (jnp.minimum((i + 1) * per_tile, last), 0)),
                  pl.BlockSpec((1, width), lambda i: (0, 0))],
        out_specs=pl.BlockSpec((tm, width), lambda i: (i, 0)),
        out_shape=jax.ShapeDtypeStruct((m, width), F32),
        compiler_params=_cparams(("parallel",)),
    )(z, z, z, mu.reshape(1, width))


def _even_mixer(u, b, t, n_ctx, w_in, mu, w0, w2, a0, a2, g2, k_k, k_a, r_k, gn_g, gn_b, lower,
                hg_norm_g, w_out):
    za, zb = matmul(u, w_in, splits=(RW_IN, HG_IN))
    za = token_shift(za, mu, t, n_ctx).reshape(b, t, RW_IN)
    zb = zb.reshape(b, t, HG_IN)
    yf, yb = rwkv7_scan(za, w0, w2, a0, a2, k_k, k_a, n_ctx)
    of, ob = hgrn2_scan(zb, lower, n_ctx)
    flat = lambda x: x.reshape(b * t, x.shape[-1])
    y = even_post(flat(za), flat(yf), flat(yb), flat(zb), flat(of), flat(ob), a0, a2, g2, k_a,
                  r_k.reshape(-1), gn_g, gn_b, hg_norm_g, w_out)
    return y


def _rope_tables(rows):
    t = jnp.arange(rows * GRID_W)
    row = (t // GRID_W).astype(F32)
    col = (t % GRID_W).astype(F32)
    half = MLA_ROPE // 2
    inv_freq = ROPE_BASE ** (-jnp.arange(0, half, 2, dtype=F32) / half)
    ang_r = row[:, None] * inv_freq
    ang_c = col[:, None] * inv_freq
    ang = jnp.concatenate([ang_r, ang_r, ang_c, ang_c], axis=-1)
    return jnp.cos(ang), jnp.sin(ang)


def _mla_weights(w_in, w_qb, w_kvb, w_out):
    hh, half = MLA_HEADS, MLA_HEADS // 2
    w_in_ext = jnp.concatenate([w_in, _rot_half(w_in[:, MLA_Q_RANK + MLA_KV_RANK:])], axis=1)
    kvb = w_kvb.reshape(MLA_KV_RANK, half, 2, 2, MLA_NOPE)
    kv128 = jnp.stack([kvb[:, :, 0], kvb[:, :, 1, ::-1]], axis=2).reshape(MLA_KV_RANK, hh, LANES)
    kv_rows = jnp.pad(kv128, ((0, 0), (0, 0), (0, MLA_SLOT - LANES))).reshape(MLA_KV_RANK, hh * MLA_SLOT)
    rope_rows = jnp.pad(jnp.eye(MLA_ROPE, dtype=F32), ((0, 0), (LANES, MLA_SLOT - LANES - MLA_ROPE)))
    w_kv_slots = jnp.concatenate([kv_rows, jnp.tile(rope_rows, (1, hh))], axis=0).astype(BF16)
    qb = w_qb.reshape(MLA_Q_RANK, hh, MLA_QK)
    nope = qb[..., :MLA_NOPE].reshape(MLA_Q_RANK, half, 2, MLA_NOPE)
    zeros = jnp.zeros_like(nope[:, :, 0])
    nope128 = jnp.stack([jnp.concatenate([nope[:, :, 0], zeros], -1),
                         jnp.concatenate([zeros, nope[:, :, 1]], -1)], axis=2).reshape(MLA_Q_RANK, hh, LANES)
    rope = qb[..., MLA_NOPE:]
    pad_rope = lambda x: jnp.pad(x, ((0, 0), (0, 0), (0, LANES - MLA_ROPE)))
    wq_slots = jnp.concatenate([nope128, pad_rope(rope)], -1).reshape(MLA_Q_RANK, hh * MLA_SLOT)
    wr_slots = pad_rope(_rot_half(rope)).reshape(MLA_Q_RANK, hh * LANES)
    w_out_perm = w_out.reshape(half, 2, MLA_V, -1)[:, ::-1].reshape(hh * MLA_V, -1)
    return w_in_ext, w_kv_slots, wq_slots.astype(BF16), wr_slots.astype(BF16), w_out_perm


def _mla_mixer(u, b, t, n_ctx, w_in, q_norm_g, w_qb, kv_norm_g, w_kvb, w_out, cos, sin):
    w_in_ext, w_kv_slots, wq_slots, wr_slots, w_out_perm = _mla_weights(w_in, w_qb, w_kvb, w_out)
    zq, z_kv = matmul(u, w_in_ext, splits=(MLA_Q_RANK, MLA_KV_RANK + 2 * MLA_ROPE))
    ones, zeros = jnp.ones((n_ctx, MLA_ROPE), F32), jnp.zeros((n_ctx, MLA_ROPE), F32)
    cs = jnp.concatenate([jnp.concatenate([ones, cos], axis=0), jnp.concatenate([zeros, sin], axis=0)], axis=1)
    kv_slots = mla_kv(z_kv, cs, kv_norm_g, w_kv_slots, t).reshape(b, t, MLA_HEADS * MLA_SLOT)
    zq = zq.reshape(b, t, MLA_Q_RANK)
    widen = lambda x, fill: jnp.pad(x, ((0, 0), (0, LANES - MLA_ROPE)), constant_values=fill)
    o_ctx = mla_attention(zq[:, :n_ctx], kv_slots, n_ctx, widen(ones, 1.0), widen(zeros, 0.0), q_norm_g,
                          wq_slots, wr_slots)
    o_lat = mla_attention(zq[:, n_ctx:], kv_slots, t, widen(cos, 1.0), widen(sin, 0.0), q_norm_g,
                          wq_slots, wr_slots)
    o = jnp.concatenate([o_ctx, o_lat], axis=1).reshape(b * t, MLA_HEADS * MLA_V)
    return matmul(o, w_out_perm)


def kernel(x, c, ctx, c_ctx, mod_w, mod_b, ln_g, ln_b, ev_w_in, rw_mu, rw_w0, rw_w2, rw_a0, rw_a2,
           rw_g2, rw_k_k, rw_k_a, rw_r_k, rw_gn_g, rw_gn_b, hg_lb, hg_norm_g, ev_w_out, od_w_in,
           mla_q_norm_g, mla_w_qb, mla_kv_norm_g, mla_w_kvb, od_w_out, moe_w_router, moe_b_router,
           moe_w_gu, moe_b_gu, moe_w_down, moe_b_down):
    b, n_lat, dm = x.shape
    n_ctx = ctx.shape[1]
    t = n_ctx + n_lat
    cos, sin = _rope_tables(n_lat // GRID_W)
    lb = jax.nn.softmax(hg_lb.astype(F32), axis=0)
    hg_lower = jnp.cumsum(lb, axis=0) - lb[0]
    c_act = c * jax.nn.sigmoid(c)
    cc_act = c_ctx * jax.nn.sigmoid(c_ctx)
    mod_in = jnp.concatenate([c_act, cc_act[None]], axis=0)
    pad = (-mod_in.shape[0]) % 8
    mod_in = jnp.pad(mod_in, ((0, pad), (0, 0)))
    mods = jnp.stack([matmul(mod_in, mod_w[layer]) + mod_b[layer] for layer in range(DEPTH)])
    mods = mods.reshape(DEPTH * mod_in.shape[0], 1, N_MOD * dm)
    nm = functools.partial(norm_mod, mods=mods, rows_per_seq=t, n_ctx=n_ctx, n_batch=b)
    h = jnp.concatenate([ctx, x], axis=1).reshape(b * t, dm)
    u16 = nm(h, None, mod=(0, 0, 1), outs=('u16',))
    for layer in range(DEPTH):
        j = layer // 2
        if layer % 2 == 0:
            y = _even_mixer(u16, b, t, n_ctx, ev_w_in[j], rw_mu[j], rw_w0[j], rw_w2[j], rw_a0[j],
                            rw_a2[j], rw_g2[j], rw_k_k[j], rw_k_a[j], rw_r_k[j], rw_gn_g[j], rw_gn_b[j],
                            hg_lower[j], hg_norm_g[j], ev_w_out[j])
        else:
            y = _mla_mixer(u16, b, t, n_ctx, od_w_in[j], mla_q_norm_g[j], mla_w_qb[j],
                           mla_kv_norm_g[j], mla_w_kvb[j], od_w_out[j], cos, sin)
        h, u32, u16 = nm(h, y, gate=(layer, 2), ln=(ln_g[layer, 0], ln_b[layer, 0]), mod=(layer, 3, 4),
                         outs=('h', 'u32', 'u16'))
        f = moe(u32, u16, layer, moe_w_router[layer], moe_b_router[layer], moe_w_gu, moe_b_gu,
                moe_w_down, moe_b_down)
        if layer + 1 < DEPTH:
            h, u16 = nm(h, f, gate=(layer, 5), ln=(ln_g[layer, 1], ln_b[layer, 1]),
                        mod=(layer + 1, 0, 1), outs=('h', 'u16'))
        else:
            h = nm(h, f, gate=(layer, 5), ln=(ln_g[layer, 1], ln_b[layer, 1]), outs=('h',))
    return h.reshape(b, t, dm)[:, n_ctx:]
```

```python
import functools

import jax
import jax.numpy as jnp
from jax import lax
from jax.experimental import pallas as pl
from jax.experimental.pallas import tpu as pltpu

F32 = jnp.float32
BF16 = jnp.bfloat16
HIGHEST = lax.Precision.HIGHEST

DEPTH = 4
GRID_W = 64
N_MOD = 6

RW_HEADS = 8
RW_HEAD_DIM = 64
RW_DIM = RW_HEADS * RW_HEAD_DIM
RW_DECAY_LORA = 64
RW_ICLR_LORA = 64
RW_GATE_LORA = 128
RW_GN_EPS = 64e-5
RW_IN = 3 * RW_DIM + 2 * RW_DECAY_LORA + 2 * RW_ICLR_LORA + RW_GATE_LORA
RW_LORA_OFF = 3 * RW_DIM

HG_HEADS = 4
HG_KEY_DIM = 128
HG_DIM = HG_HEADS * HG_KEY_DIM
HG_VDIM = HG_DIM
HG_IN = 3 * HG_DIM + 2 * HG_VDIM
HG_F_FLOOR = 1e-30

MLA_HEADS = 16
MLA_NOPE = 64
MLA_ROPE = 32
MLA_V = 64
MLA_Q_RANK = 256
MLA_KV_RANK = 128
MLA_QK = MLA_NOPE + MLA_ROPE
MLA_SCALE = MLA_QK ** -0.5
ROPE_BASE = 10000.0

N_EXPERTS = 32
TOP_K = 4
EXPERT_DIM = 1024
SWIGLU_LIMIT = 7.0
SWIGLU_ALPHA = 1.702

DEEPNORM_ALPHA = (2 * DEPTH) ** 0.25
LN_EPS = 1e-5
RMS_EPS = 1e-6

LANES = 128
SCAN_CHUNK = 64
HG_SUB = 16
MOE_TILE = 512
RANK_BLOCK = 256
VMEM_LIMIT = 56 * 1024 * 1024


def _cparams(sem):
    return pltpu.CompilerParams(dimension_semantics=sem, vmem_limit_bytes=VMEM_LIMIT)


def _row_tile(m, tm):
    if m <= tm:
        return m
    while m % tm:
        tm //= 2
    assert tm >= 8
    return tm


def _mm_kernel(x_ref, w_ref, *o_refs):
    acc = jnp.dot(x_ref[...].astype(BF16), w_ref[...], preferred_element_type=F32)
    off = 0
    for o_ref in o_refs:
        width = o_ref.shape[1]
        o_ref[...] = acc[:, off:off + width].astype(o_ref.dtype)
        off += width


def matmul(x, w, splits=None, out_dtype=F32, tm=512):
    m, k = x.shape
    n = w.shape[1]
    tm = _row_tile(m, tm)
    widths = (n,) if splits is None else splits
    assert sum(widths) == n
    outs = pl.pallas_call(
        _mm_kernel,
        grid=(m // tm,),
        in_specs=[pl.BlockSpec((tm, k), lambda i: (i, 0)),
                  pl.BlockSpec((k, n), lambda i: (0, 0))],
        out_specs=[pl.BlockSpec((tm, wd), lambda i: (i, 0)) for wd in widths],
        out_shape=[jax.ShapeDtypeStruct((m, wd), out_dtype) for wd in widths],
        compiler_params=_cparams(("parallel",)),
    )(x, w.astype(BF16))
    return outs[0] if splits is None else outs


def _router_kernel(x_ref, w_ref, b_ref, idx_ref, gate_ref):
    x = jnp.dot(x_ref[...], w_ref[...], preferred_element_type=F32, precision=HIGHEST) + b_ref[...]
    tm, ne = x.shape
    lane = lax.broadcasted_iota(jnp.int32, (tm, ne), 1).astype(F32)
    slot = lax.broadcasted_iota(jnp.int32, (tm, LANES), 1)
    idx = jnp.zeros((tm, LANES), F32)
    tops = []
    for k in range(TOP_K):
        mx = jnp.max(x, axis=-1, keepdims=True)
        ix = jnp.min(jnp.where(x == mx, lane, float(ne)), axis=-1, keepdims=True)
        tops.append(mx)
        idx = jnp.where(slot == k, ix, idx)
        x = jnp.where(lane == ix, -jnp.inf, x)
    e = [jnp.exp(v - tops[0]) for v in tops]
    inv = 1.0 / sum(e)
    gates = jnp.zeros((tm, LANES), F32)
    for k in range(TOP_K):
        gates = jnp.where(slot == k, e[k] * inv, gates)
    idx_ref[...] = idx.astype(jnp.int32)
    gate_ref[...] = gates


def router_top4(x, w, b, tm=512):
    m, k = x.shape
    n = w.shape[1]
    tm = _row_tile(m, tm)
    return pl.pallas_call(
        _router_kernel,
        grid=(m // tm,),
        in_specs=[pl.BlockSpec((tm, k), lambda i: (i, 0)),
                  pl.BlockSpec((k, n), lambda i: (0, 0)),
                  pl.BlockSpec((1, n), lambda i: (0, 0))],
        out_specs=[pl.BlockSpec((tm, LANES), lambda i: (i, 0))] * 2,
        out_shape=[jax.ShapeDtypeStruct((m, LANES), jnp.int32), jax.ShapeDtypeStruct((m, LANES), F32)],
        compiler_params=_cparams(("parallel",)),
    )(x, w, b.reshape(1, n))


def _bwd_chunk(c, n_ctx_chunks, n_chunks):
    return jnp.where(c < n_ctx_chunks, n_ctx_chunks - 1 - c, n_chunks - 1 - (c - n_ctx_chunks))


def _dot(a, b):
    return jnp.dot(a.astype(BF16), b.astype(BF16), preferred_element_type=F32)


def _dot_nt(a, b):
    return lax.dot_general(a.astype(BF16), b.astype(BF16), (((1,), (1,)), ((), ())),
                           preferred_element_type=F32)


def _dot_tn(a, b):
    return lax.dot_general(a.astype(BF16), b.astype(BF16), (((0,), (0,)), ((), ())),
                           preferred_element_type=F32)


def _dot_f32(a, b):
    return jnp.dot(a, b, preferred_element_type=F32, precision=HIGHEST)


def _seg_sum(x, ones_bd):
    hi = x.astype(BF16)
    lo = (x - hi.astype(F32)).astype(BF16)
    return (jnp.dot(hi, ones_bd, preferred_element_type=F32)
            + jnp.dot(lo, ones_bd, preferred_element_type=F32))


def _block_diag_ones(width, block):
    i = jnp.arange(width) // block
    return (i[:, None] == i[None, :]).astype(BF16)


def _hgrn_kernel(zf_ref, zb_ref, lo_ref, bd_ref, of_ref, ob_ref, st_ref):
    c = pl.program_id(1)
    n, sub = SCAN_CHUNK, HG_SUB
    nb = n // sub

    @pl.when(c == 0)
    def _():
        st_ref[...] = jnp.zeros_like(st_ref)

    row = lax.broadcasted_iota(jnp.int32, (n, n), 0)
    col = lax.broadcasted_iota(jnp.int32, (n, n), 1)
    colb = lax.broadcasted_iota(jnp.int32, (sub, LANES), 1)
    rown = lax.broadcasted_iota(jnp.int32, (sub, n), 0)
    coln = lax.broadcasted_iota(jnp.int32, (sub, n), 1)
    bd2 = bd_ref[...]

    dirs = []
    for d, z_ref in ((0, zf_ref), (1, zb_ref)):
        z = z_ref[0]
        lower = lo_ref[d]
        qraw = z[:, :HG_DIM]
        sig = jax.nn.sigmoid(z[:, HG_DIM * (1 + d):HG_DIM * (2 + d)])
        v = z[:, 3 * HG_DIM:3 * HG_DIM + HG_VDIM]
        f = lower + (1.0 - lower) * sig
        g = jnp.log(jnp.maximum(f, HG_F_FLOOR))
        kin = (1.0 - lower) * (1.0 - sig)
        q = qraw * jax.nn.sigmoid(qraw)
        lag = (row - col) if d == 0 else (col - row)
        cum = _dot_f32((lag >= 0).astype(F32), g)
        cum_ex = cum - g
        tot = jnp.sum(g, axis=0, keepdims=True)
        blocks = []
        for j in range(nb):
            lo, hi = j * sub, (j + 1) * sub
            q_j, cum_j = q[lo:hi], cum[lo:hi]
            ref_j = cum_ex[lo:lo + 1] if d == 0 else cum_ex[hi - 1:hi]
            qt = q_j * jnp.exp(cum_j - ref_j)
            kt = kin * jnp.exp(jnp.minimum(ref_j - cum, 0.0))
            pair = jnp.concatenate(
                [q_j * kin[s:s + 1] * jnp.exp(jnp.minimum(cum_j - cum[s:s + 1], 0.0))
                 for s in range(lo, hi)], axis=0)
            blocks.append((qt, kt, pair))
        dirs.append(dict(d=d, v=v, qe=q * jnp.exp(cum), kdec=kin * jnp.exp(tot - cum),
                         etot=jnp.exp(tot), blocks=blocks))

    for dd, o_ref in zip(dirs, (of_ref, ob_ref)):
        d = dd['d']
        for j, (qt, kt, pair) in enumerate(dd['blocks']):
            lo, hi = j * sub, (j + 1) * sub
            sums = [_dot(pair[:, 2 * LANES * hp:2 * LANES * (hp + 1)], bd2) for hp in range(HG_HEADS // 2)]
            for h in range(HG_HEADS):
                hs = slice(LANES * h, LANES * (h + 1))
                rs = sums[h // 2][:, LANES * (h % 2):LANES * (h % 2 + 1)]
                diag = jnp.zeros((sub, LANES), F32)
                for s in range(sub):
                    diag = jnp.where(colb == lo + s, rs[s * sub:(s + 1) * sub], diag)
                off = _dot_nt(qt[:, hs], kt[:, hs])
                if d == 0:
                    earlier, ordered = coln < lo, rown + lo >= coln
                else:
                    earlier, ordered = coln >= hi, rown + lo <= coln
                sc = jnp.where(earlier, off, jnp.where(ordered, diag[:, :n], 0.0))
                dd.setdefault(('out', h), []).append(_dot(sc, dd['v'][:, hs]))

    for dd, o_ref in zip(dirs, (of_ref, ob_ref)):
        d = dd['d']
        for h in range(HG_HEADS):
            hs = slice(LANES * h, LANES * (h + 1))
            st = st_ref[d * HG_HEADS + h]
            out = jnp.concatenate(dd[('out', h)], axis=0) + _dot_nt(dd['qe'][:, hs], st)
            o_ref[0, :, hs] = out
            st_ref[d * HG_HEADS + h] = (st * dd['etot'][:, hs]
                                        + _dot_tn(dd['v'][:, hs], dd['kdec'][:, hs]))


def hgrn2_scan(zb, lower, n_ctx):
    b, t, _ = zb.shape
    n = SCAN_CHUNK
    nc, ncc = t // n, n_ctx // n
    bwd = functools.partial(_bwd_chunk, n_ctx_chunks=ncc, n_chunks=nc)
    const = lambda shape: pl.BlockSpec(shape, lambda bi, c: (0,) * len(shape))
    return pl.pallas_call(
        _hgrn_kernel,
        grid=(b, nc),
        in_specs=[pl.BlockSpec((1, n, HG_IN), lambda bi, c: (bi, c, 0)),
                  pl.BlockSpec((1, n, HG_IN), lambda bi, c: (bi, bwd(c), 0)),
                  const((2, 1, HG_DIM)), const((2 * LANES, 2 * LANES))],
        out_specs=[pl.BlockSpec((1, n, HG_VDIM), lambda bi, c: (bi, c, 0)),
                   pl.BlockSpec((1, n, HG_VDIM), lambda bi, c: (bi, bwd(c), 0))],
        out_shape=[jax.ShapeDtypeStruct((b, t, HG_VDIM), F32)] * 2,
        scratch_shapes=[pltpu.VMEM((2 * HG_HEADS, HG_KEY_DIM, HG_KEY_DIM), F32)],
        compiler_params=_cparams(("parallel", "arbitrary")),
    )(zb, zb, lower.reshape(2, 1, HG_DIM), _block_diag_ones(2 * LANES, LANES))


def _rwkv_iclr(z, d, a0_ref, a2_ref):
    ad = z[:, RW_LORA_OFF + LANES:RW_LORA_OFF + 2 * LANES]
    return jax.nn.sigmoid(a0_ref[d] + _dot(ad, a2_ref[d]))


def _rwkv_scan_kernel(zf_ref, zb_ref, w0_ref, w2_ref, a0_ref, a2_ref, kk_ref, ka_ref, bd_ref,
                      yf_ref, yb_ref, st_ref):
    c = pl.program_id(1)
    n, cdim = SCAN_CHUNK, RW_DIM
    npair = cdim // LANES

    @pl.when(c == 0)
    def _():
        st_ref[...] = jnp.zeros_like(st_ref)

    row = lax.broadcasted_iota(jnp.int32, (n, n), 0)
    col = lax.broadcasted_iota(jnp.int32, (n, n), 1)
    row4 = lax.broadcasted_iota(jnp.int32, (2 * n, 4 * n), 0) % n
    col4 = lax.broadcasted_iota(jnp.int32, (2 * n, 4 * n), 1) % n
    lane = lax.broadcasted_iota(jnp.int32, (1, LANES), 1)
    half = [lane < RW_HEAD_DIM, lane >= RW_HEAD_DIM]

    def stack2(x):
        return jnp.concatenate([jnp.where(half[0], x, 0.0), jnp.where(half[1], x, 0.0)], axis=0)

    r2 = lax.broadcasted_iota(jnp.int32, (LANES, LANES), 0) < RW_HEAD_DIM
    c2 = lax.broadcasted_iota(jnp.int32, (LANES, LANES), 1) < RW_HEAD_DIM
    same_head = r2 == c2

    chains = []
    for d, z_ref in ((0, zf_ref), (1, zb_ref)):
        z = z_ref[0]
        r, k, v = z[:, :cdim], z[:, cdim:2 * cdim], z[:, 2 * cdim:3 * cdim]
        wd = z[:, RW_LORA_OFF:RW_LORA_OFF + LANES]
        w_pre = w0_ref[d] + _dot(jnp.tanh(wd), w2_ref[d])
        lw = -jnp.exp(-0.5) * jax.nn.sigmoid(w_pre)
        a = _rwkv_iclr(z, d, a0_ref, a2_ref)
        kk = k * kk_ref[...]
        kk = kk * lax.rsqrt(jnp.maximum(_seg_sum(kk * kk, bd_ref[...]), 1e-24))
        kd = k * (1.0 + (a - 1.0) * ka_ref[...])
        bb = kk * a
        lag = (row - col) if d == 0 else (col - row)
        lag4 = (row4 - col4) if d == 0 else (col4 - row4)
        incl4, strict4 = lag4 >= 0, lag4 > 0
        cum = _dot_f32((lag >= 0).astype(F32), lw)
        tot = jnp.sum(lw, axis=0, keepdims=True)
        g_inv = jnp.exp(-cum)
        g_tail = jnp.exp(tot - cum)
        rt = r * jnp.exp(cum)
        at = -kk * jnp.exp(cum - lw)
        etot = jnp.exp(tot)
        bt, kt = bb * g_inv, kd * g_inv
        tail = jnp.concatenate([bb * g_tail, kd * g_tail], axis=0)
        for p in range(npair):
            ps = slice(LANES * p, LANES * (p + 1))
            chains.append(dict(d=d, p=p, ps=ps, incl4=incl4, strict4=strict4, tail=tail[:, ps],
                               at=stack2(at[:, ps]), rt=rt[:, ps], bt=bt[:, ps], kt=kt[:, ps],
                               v=v[:, ps], etot=etot[:, ps]))

    for ch in chains:
        left = jnp.concatenate([ch['at'], stack2(ch['rt'])], axis=0)
        right = jnp.concatenate([stack2(ch['bt']), stack2(ch['kt'])], axis=0)
        prod = _dot_nt(left, right)
        ch['top'] = jnp.where(ch['strict4'], prod[:2 * n], 0.0)
        ch['bot'] = jnp.where(ch['incl4'], prod[2 * n:], 0.0)
    for ch in chains:
        ch['vs'] = stack2(ch['v'])
        ch['x'] = jnp.concatenate([ch['at'], _dot(ch['top'][:, 2 * n:], ch['vs'])], axis=1)
        ch['a'] = ch['top'][:, :2 * n]
    steps = n.bit_length() - 1
    for i in range(steps):
        for ch in chains:
            ch['x'] = ch['x'] + _dot(ch['a'], ch['x'])
            if i + 1 < steps:
                ch['a'] = _dot(ch['a'], ch['a'])
    for ch in chains:
        x = ch['x'][:n] + ch['x'][n:]
        st = st_ref[ch['d'] * npair + ch['p']]
        ws = _dot_nt(jnp.concatenate([x[:, :LANES], ch['rt']], axis=0), st)
        u = ws[:n] + x[:, LANES:]
        ys = _dot(ch['bot'], jnp.concatenate([stack2(u), ch['vs']], axis=0))
        y_ref = yf_ref if ch['d'] == 0 else yb_ref
        y_ref[0, :, ch['ps']] = ws[n:] + ys[:n] + ys[n:]
        delta = _dot_tn(jnp.concatenate([u, ch['v']], axis=0), ch['tail'])
        st_ref[ch['d'] * npair + ch['p']] = st * ch['etot'] + jnp.where(same_head, delta, 0.0)


def _pad_lora(w, d):
    return jnp.concatenate([w[d] if i == d else jnp.zeros_like(w[i]) for i in range(2)], axis=0)


def rwkv7_scan(za, w0, w2, a0, a2, k_k, k_a, n_ctx):
    b, t, _ = za.shape
    n, cdim = SCAN_CHUNK, RW_DIM
    nc, ncc = t // n, n_ctx // n
    bwd = functools.partial(_bwd_chunk, n_ctx_chunks=ncc, n_chunks=nc)
    const = lambda shape: pl.BlockSpec(shape, lambda bi, c: (0,) * len(shape))
    w2p = jnp.stack([_pad_lora(w2, d) for d in range(2)]).astype(BF16)
    a2p = jnp.stack([_pad_lora(a2, d) for d in range(2)]).astype(BF16)
    return pl.pallas_call(
        _rwkv_scan_kernel,
        grid=(b, nc),
        in_specs=[pl.BlockSpec((1, n, RW_IN), lambda bi, c: (bi, c, 0)),
                  pl.BlockSpec((1, n, RW_IN), lambda bi, c: (bi, bwd(c), 0)),
                  const((2, 1, cdim)), const((2, LANES, cdim)), const((2, 1, cdim)),
                  const((2, LANES, cdim)), const((1, cdim)), const((1, cdim)), const((cdim, cdim))],
        out_specs=[pl.BlockSpec((1, n, cdim), lambda bi, c: (bi, c, 0)),
                   pl.BlockSpec((1, n, cdim), lambda bi, c: (bi, bwd(c), 0))],
        out_shape=[jax.ShapeDtypeStruct((b, t, cdim), F32)] * 2,
        scratch_shapes=[pltpu.VMEM((2 * cdim // LANES, LANES, LANES), F32)],
        compiler_params=_cparams(("parallel", "arbitrary")),
    )(za, za, w0.reshape(2, 1, cdim), w2p, a0.reshape(2, 1, cdim), a2p, k_k.reshape(1, cdim),
      k_a.reshape(1, cdim), _block_diag_ones(cdim, RW_HEAD_DIM))


def _even_post_kernel(za_ref, yf_ref, yb_ref, zg_ref, of_ref, ob_ref, a0_ref, a2_ref, g2_ref,
                      ka_ref, rk_ref, gng_ref, gnb_ref, hgg_ref, bd64_ref, bd128_ref, wout_ref,
                      o_ref):
    cdim = RW_DIM
    z = za_ref[...]
    r, k, v = z[:, :cdim], z[:, cdim:2 * cdim], z[:, 2 * cdim:3 * cdim]
    a_sum = _rwkv_iclr(z, 0, a0_ref, a2_ref) + _rwkv_iclr(z, 1, a0_ref, a2_ref)
    k_sum = k * (2.0 + (a_sum - 2.0) * ka_ref[...])
    gate = _dot(jax.nn.sigmoid(z[:, RW_LORA_OFF + 2 * LANES:]), g2_ref[...])
    bd64 = bd64_ref[...]
    y = yf_ref[...] + yb_ref[...]
    mean = _seg_sum(y, bd64) * (1.0 / RW_HEAD_DIM)
    cen = y - mean
    var = _seg_sum(cen * cen, bd64) * (1.0 / RW_HEAD_DIM)
    y = cen * lax.rsqrt(var + RW_GN_EPS) * gng_ref[...] + gnb_ref[...]
    bonus = _seg_sum(r * k_sum * rk_ref[...], bd64) * v
    y_rw = (y + bonus) * gate

    o = of_ref[...] + ob_ref[...]
    ms = _seg_sum(o * o, bd128_ref[...]) * (1.0 / HG_KEY_DIM)
    g_hg = zg_ref[...]
    y_hg = o * lax.rsqrt(ms + RMS_EPS) * hgg_ref[...] * (g_hg * jax.nn.sigmoid(g_hg))

    y_all = jnp.concatenate([y_rw, y_hg], axis=1).astype(BF16)
    o_ref[...] = jnp.dot(y_all, wout_ref[...], preferred_element_type=F32)


def even_post(za, yf, yb, zb, of, ob, a0, a2, g2, k_a, r_k, gn_g, gn_b, hg_norm_g, w_out, tm=256):
    m = za.shape[0]
    cdim, dm = RW_DIM, w_out.shape[1]
    tm = _row_tile(m, tm)
    rows = lambda width, blk=0: pl.BlockSpec((tm, width), lambda i: (i, blk))
    const = lambda shape: pl.BlockSpec(shape, lambda i: (0,) * len(shape))
    a2p = jnp.stack([_pad_lora(a2, d) for d in range(2)]).astype(BF16)
    vec = lambda x: x.reshape(1, cdim)
    return pl.pallas_call(
        _even_post_kernel,
        grid=(m // tm,),
        in_specs=[rows(RW_IN), rows(cdim), rows(cdim), rows(HG_VDIM, (HG_IN - HG_VDIM) // HG_VDIM),
                  rows(HG_VDIM), rows(HG_VDIM),
                  const((2, 1, cdim)), const((2, LANES, cdim)), const((RW_GATE_LORA, cdim)),
                  const((1, cdim)), const((1, cdim)), const((1, cdim)), const((1, cdim)),
                  const((1, HG_VDIM)), const((cdim, cdim)), const((HG_VDIM, HG_VDIM)),
                  const((cdim + HG_VDIM, dm))],
        out_specs=rows(dm),
        out_shape=jax.ShapeDtypeStruct((m, dm), F32),
        compiler_params=_cparams(("parallel",)),
    )(za, yf, yb, zb, of, ob, a0.reshape(2, 1, cdim), a2p, g2.astype(BF16), vec(k_a), vec(r_k),
      vec(gn_g), vec(gn_b), hg_norm_g.reshape(1, HG_VDIM), _block_diag_ones(cdim, RW_HEAD_DIM),
      _block_diag_ones(HG_VDIM, HG_KEY_DIM), w_out.astype(BF16))


MLA_SLOT = 2 * LANES
MLA_GROUP = 4


def _rot_half(w):
    r1, r2, c1, c2 = jnp.split(w, 4, axis=-1)
    return jnp.concatenate([-r2, r1, -c2, c1], axis=-1)


def _mla_kv_kernel(z_ref, cs_ref, g_ref, w_ref, o_ref):
    z = z_ref[...]
    kva = z[:, :MLA_KV_RANK]
    kva = kva * lax.rsqrt(jnp.mean(kva * kva, axis=-1, keepdims=True) + RMS_EPS) * g_ref[...]
    cs = cs_ref[...]
    kpe = (z[:, MLA_KV_RANK:MLA_KV_RANK + MLA_ROPE] * cs[:, :MLA_ROPE]
           + z[:, MLA_KV_RANK + MLA_ROPE:] * cs[:, MLA_ROPE:])
    x = jnp.concatenate([kva, kpe], axis=1).astype(BF16)
    o_ref[...] = jnp.dot(x, w_ref[...], preferred_element_type=F32).astype(o_ref.dtype)


def mla_kv(z_kv, cs, kv_norm_g, w_kv_slots, rows_per_seq, tm=256):
    m, width = z_kv.shape
    tm = _row_tile(rows_per_seq, tm)
    per_seq = rows_per_seq // tm
    n_out = w_kv_slots.shape[1]
    return pl.pallas_call(
        _mla_kv_kernel,
        grid=(m // tm,),
        in_specs=[pl.BlockSpec((tm, width), lambda i: (i, 0)),
                  pl.BlockSpec((tm, 2 * MLA_ROPE), lambda i: (i % per_seq, 0)),
                  pl.BlockSpec((1, MLA_KV_RANK), lambda i: (0, 0)),
                  pl.BlockSpec(w_kv_slots.shape, lambda i: (0, 0))],
        out_specs=pl.BlockSpec((tm, n_out), lambda i: (i, 0)),
        out_shape=jax.ShapeDtypeStruct((m, n_out), BF16),
        compiler_params=_cparams(("parallel",)),
    )(z_kv, cs, kv_norm_g.reshape(1, MLA_KV_RANK), w_kv_slots)


def _mla_attn_kernel(zq_ref, kv_ref, cos_ref, sin_ref, g_ref, wq_ref, wr_ref, o_ref):
    zq = zq_ref[0]
    zn = (zq * lax.rsqrt(jnp.mean(zq * zq, axis=-1, keepdims=True) + RMS_EPS) * g_ref[...]).astype(BF16)
    a = jnp.dot(zn, wq_ref[...], preferred_element_type=F32)
    ar = jnp.dot(zn, wr_ref[...], preferred_element_type=F32)
    cos, sin = cos_ref[...], sin_ref[...]
    low = lax.broadcasted_iota(jnp.int32, (1, LANES), 1) < MLA_V
    blocks = []
    for pair in range(MLA_GROUP // 2):
        res = []
        for j in range(2):
            h = 2 * pair + j
            q_nope = a[:, MLA_SLOT * h:MLA_SLOT * h + LANES]
            q_rope = (a[:, MLA_SLOT * h + LANES:MLA_SLOT * (h + 1)] * cos
                      + ar[:, LANES * h:LANES * (h + 1)] * sin)
            qf = (jnp.concatenate([q_nope, q_rope], axis=1) * MLA_SCALE).astype(BF16)
            s = lax.dot_general(qf, kv_ref[0, :, MLA_SLOT * h:MLA_SLOT * (h + 1)],
                                (((1,), (1,)), ((), ())), preferred_element_type=F32)
            p = jnp.exp(s - jnp.max(s, axis=-1, keepdims=True))
            l = jnp.sum(p, axis=-1, keepdims=True)
            res.append(jnp.dot(p.astype(BF16), kv_ref[0, :, MLA_SLOT * h:MLA_SLOT * h + LANES],
                               preferred_element_type=F32) / l)
        blocks.append(jnp.where(low, res[1], res[0]))
    o_ref[0] = jnp.concatenate(blocks, axis=1).astype(o_ref.dtype)


def mla_attention(zq, kv_slots, n_keys, cos, sin, q_norm_g, wq_slots, wr_slots, tq=512):
    b, nq, _ = zq.shape
    tq = _row_tile(nq, tq)
    gw = MLA_GROUP * MLA_SLOT
    return pl.pallas_call(
        _mla_attn_kernel,
        grid=(b, MLA_HEADS // MLA_GROUP, nq // tq),
        in_specs=[pl.BlockSpec((1, tq, MLA_Q_RANK), lambda bi, g, qi: (bi, qi, 0)),
                  pl.BlockSpec((1, n_keys, gw), lambda bi, g, qi: (bi, 0, g)),
                  pl.BlockSpec((tq, LANES), lambda bi, g, qi: (qi, 0)),
                  pl.BlockSpec((tq, LANES), lambda bi, g, qi: (qi, 0)),
                  pl.BlockSpec((1, MLA_Q_RANK), lambda bi, g, qi: (0, 0)),
                  pl.BlockSpec((MLA_Q_RANK, gw), lambda bi, g, qi: (0, g)),
                  pl.BlockSpec((MLA_Q_RANK, gw // 2), lambda bi, g, qi: (0, g))],
        out_specs=pl.BlockSpec((1, tq, MLA_GROUP * MLA_V), lambda bi, g, qi: (bi, qi, g)),
        out_shape=jax.ShapeDtypeStruct((b, nq, MLA_HEADS * MLA_V), BF16),
        compiler_params=_cparams(("parallel", "parallel", "parallel")),
    )(zq, kv_slots, cos, sin, q_norm_g.reshape(1, MLA_Q_RANK), wq_slots, wr_slots)


def _moe_kernel(te_ref, nt_ref, x_ref, wgu_ref, bgu_ref, wd_ref, bd_ref, o_ref, wgu_bf, wd_bf):
    i = pl.program_id(0)
    live = i < nt_ref[0]
    new_expert = jnp.logical_or(i == 0, te_ref[i] != te_ref[jnp.maximum(i - 1, 0)])

    @pl.when(jnp.logical_and(live, new_expert))
    def _():
        wgu_bf[...] = wgu_ref[0, 0].astype(BF16)
        wd_bf[...] = wd_ref[0, 0].astype(BF16)

    @pl.when(live)
    def _():
        h = jnp.dot(x_ref[...], wgu_bf[...], preferred_element_type=F32) + bgu_ref[0, 0]
        glu = jnp.minimum(h[:, :EXPERT_DIM], SWIGLU_LIMIT)
        lin = jnp.clip(h[:, EXPERT_DIM:], -SWIGLU_LIMIT, SWIGLU_LIMIT)
        act = glu * jax.nn.sigmoid(SWIGLU_ALPHA * glu) * (lin + 1.0)
        y = jnp.dot(act.astype(BF16), wd_bf[...], preferred_element_type=F32) + bd_ref[0, 0]
        o_ref[...] = y.astype(o_ref.dtype)

    @pl.when(i >= nt_ref[0])
    def _():
        o_ref[...] = jnp.zeros_like(o_ref)


def moe_grouped(x_sorted, tile_expert, n_tiles_used, layer, w_gu, b_gu, w_down, b_down):
    p, dm = x_sorted.shape
    tm = MOE_TILE
    nl, e, _, f2 = w_gu.shape
    grid_spec = pltpu.PrefetchScalarGridSpec(
        num_scalar_prefetch=2,
        grid=(p // tm,),
        in_specs=[
            pl.BlockSpec((tm, dm), lambda i, te, nt: (i, 0)),
            pl.BlockSpec((1, 1, dm, f2), lambda i, te, nt: (layer, te[i], 0, 0)),
            pl.BlockSpec((1, 1, 1, f2), lambda i, te, nt: (layer, te[i], 0, 0)),
            pl.BlockSpec((1, 1, f2 // 2, dm), lambda i, te, nt: (layer, te[i], 0, 0)),
            pl.BlockSpec((1, 1, 1, dm), lambda i, te, nt: (layer, te[i], 0, 0)),
        ],
        out_specs=pl.BlockSpec((tm, dm), lambda i, te, nt: (i, 0)),
        scratch_shapes=[pltpu.VMEM((dm, f2), BF16), pltpu.VMEM((f2 // 2, dm), BF16)],
    )
    return pl.pallas_call(
        _moe_kernel,
        grid_spec=grid_spec,
        out_shape=jax.ShapeDtypeStruct((p, dm), F32),
        compiler_params=_cparams(("arbitrary",)),
    )(tile_expert, n_tiles_used, x_sorted, w_gu, b_gu.reshape(nl, e, 1, f2), w_down,
      b_down.reshape(nl, e, 1, dm))


def moe(u, u16, layer, w_router, b_router, w_gu, b_gu, w_down, b_down):
    n, dm = u.shape
    tm = MOE_TILE
    idx_lanes, gate_lanes = router_top4(u, w_router, b_router)
    top_idx = idx_lanes[:, :TOP_K]
    e_flat = top_idx.T.reshape(-1).astype(jnp.int32)
    npair = n * TOP_K
    rb = _row_tile(npair, RANK_BLOCK)
    onehot = (e_flat[:, None] == jnp.arange(N_EXPERTS, dtype=jnp.int32)[None, :])
    oh3 = onehot.astype(BF16).reshape(npair // rb, rb, N_EXPERTS)
    earlier = (jnp.arange(rb)[:, None] > jnp.arange(rb)[None, :]).astype(BF16)
    within = jnp.einsum('ij,bje->bie', earlier, oh3, preferred_element_type=F32)
    blk_tot = jnp.sum(oh3.astype(F32), axis=1)
    blk_off = jnp.cumsum(blk_tot, axis=0) - blk_tot
    counts = jnp.sum(blk_tot, axis=0).astype(jnp.int32)
    padded = ((counts + tm - 1) // tm) * tm
    ends_p = jnp.cumsum(padded)
    starts_p = ends_p - padded
    starts = jnp.cumsum(counts) - counts
    rank = within + blk_off[:, None, :] + starts_p.astype(F32)[None, None, :]
    pos = jnp.sum(oh3.astype(F32) * rank, axis=-1).astype(jnp.int32).reshape(npair)
    p_rows = npair + N_EXPERTS * tm
    n_tiles = p_rows // tm
    tile_expert = jnp.minimum(
        jnp.searchsorted(ends_p, jnp.arange(n_tiles, dtype=jnp.int32) * tm, side='right'),
        N_EXPERTS - 1).astype(jnp.int32)
    n_used = (ends_p[-1] // tm).astype(jnp.int32).reshape(1)
    _, sorted_tok = lax.sort_key_val(e_flat, jnp.arange(npair, dtype=jnp.int32) % n)
    shift = (starts - starts_p)[tile_expert]
    src = jnp.arange(p_rows, dtype=jnp.int32).reshape(n_tiles, tm) + shift[:, None]
    src_tok = sorted_tok[jnp.clip(src.reshape(-1), 0, npair - 1)]
    x_sorted = u16[src_tok]
    y_sorted = moe_grouped(x_sorted, tile_expert, n_used, layer, w_gu, b_gu, w_down, b_down)
    return y_sorted[pos].reshape(TOP_K, n, dm), gate_lanes


def _norm_mod_kernel(*refs, dm, gate, mod, has_norm, combine, outs):
    refs = list(refs)
    h = refs.pop(0)[...]
    if has_norm:
        if combine:
            y_ref, slot_gate = refs.pop(0), refs.pop(0)[...]
            y = sum(y_ref[k] * slot_gate[:, k:k + 1] for k in range(TOP_K))
        else:
            y = refs.pop(0)[...]
        mg = refs.pop(0)
        lng, lnb = refs.pop(0)[...], refs.pop(0)[...]
        x = DEEPNORM_ALPHA * h + mg[0, :, gate * dm:(gate + 1) * dm] * y
        mu = jnp.mean(x, axis=-1, keepdims=True)
        cen = x - mu
        var = jnp.mean(cen * cen, axis=-1, keepdims=True)
        h = cen * lax.rsqrt(var + LN_EPS) * lng + lnb
    if mod is not None:
        mm = refs.pop(0)
        shift, scale = mod
        u = h * (1.0 + mm[0, :, scale * dm:(scale + 1) * dm]) + mm[0, :, shift * dm:(shift + 1) * dm]
    for kind, o_ref in zip(outs, refs):
        o_ref[...] = (h if kind == 'h' else u).astype(o_ref.dtype)


def norm_mod(h, y, mods, rows_per_seq, n_ctx, n_batch, gate=None, ln=None, mod=None, outs=('h',),
             slot_gates=None, tm=256):
    m, dm = h.shape
    tm = _row_tile(n_ctx, tm)
    per_seq, ctx_tiles = rows_per_seq // tm, n_ctx // tm
    rows = mods.shape[0] // DEPTH

    def mod_row(layer):
        return lambda i: (layer * rows + jnp.where(i % per_seq < ctx_tiles, n_batch, i // per_seq), 0, 0)

    tile = pl.BlockSpec((tm, dm), lambda i: (i, 0))
    vec = pl.BlockSpec((1, dm), lambda i: (0, 0))
    args, specs = [h], [tile]
    if gate is not None:
        if slot_gates is None:
            args.append(y)
            specs.append(tile)
        else:
            args += [y, slot_gates]
            specs += [pl.BlockSpec((TOP_K, tm, dm), lambda i: (0, i, 0)),
                      pl.BlockSpec((tm, LANES), lambda i: (i, 0))]
        args += [mods, ln[0].reshape(1, dm), ln[1].reshape(1, dm)]
        specs += [pl.BlockSpec((1, 1, N_MOD * dm), mod_row(gate[0])), vec, vec]
    if mod is not None:
        args.append(mods)
        specs.append(pl.BlockSpec((1, 1, N_MOD * dm), mod_row(mod[0])))
    dtypes = {'h': F32, 'u32': F32, 'u16': BF16}
    res = pl.pallas_call(
        functools.partial(_norm_mod_kernel, dm=dm, gate=None if gate is None else gate[1],
                          mod=None if mod is None else mod[1:], has_norm=gate is not None,
                          combine=slot_gates is not None, outs=outs),
        grid=(m // tm,),
        in_specs=specs,
        out_specs=[tile] * len(outs),
        out_shape=[jax.ShapeDtypeStruct((m, dm), dtypes[k]) for k in outs],
        compiler_params=_cparams(("parallel",)),
    )(*args)
    return res[0] if len(outs) == 1 else res


SUBLANES = 8


def _shift_kernel(z_ref, prev_ref, next_ref, mu_ref, o_ref, *, per_seq, ctx_tiles):
    j = pl.program_id(0) % per_seq
    z = z_ref[...]
    tm = z.shape[0]
    starts_part = jnp.logical_or(j == 0, j == ctx_tiles)
    ends_part = jnp.logical_or(j == ctx_tiles - 1, j == per_seq - 1)
    before = prev_ref[SUBLANES - 1:SUBLANES, :] * jnp.where(starts_part, 0.0, 1.0)
    after = next_ref[0:1, :] * jnp.where(ends_part, 0.0, 1.0)
    row = lax.broadcasted_iota(jnp.int32, (tm, 1), 0)
    prev = jnp.where(row == 0, before, pltpu.roll(z, 1, axis=0))
    nxt = jnp.where(row == tm - 1, after, pltpu.roll(z, tm - 1, axis=0))
    o_ref[...] = z + mu_ref[...] * (0.5 * (prev + nxt) - z)


def token_shift(z, mu, rows_per_seq, n_ctx, tm=256):
    m, width = z.shape
    tm = _row_tile(n_ctx, tm)
    per_tile = tm // SUBLANES
    last = m // SUBLANES - 1
    return pl.pallas_call(
        functools.partial(_shift_kernel, per_seq=rows_per_seq // tm, ctx_tiles=n_ctx // tm),
        grid=(m // tm,),
        in_specs=[pl.BlockSpec((tm, width), lambda i: (i, 0)),
                  pl.BlockSpec((SUBLANES, width), lambda i: (jnp.maximum(i * per_tile - 1, 0), 0)),
                  pl.BlockSpec((SUBLANES, width), lambda i: (jnp.minimum((i + 1) * per_tile, last), 0)),
                  pl.BlockSpec((1, width), lambda i: (0, 0))],
        out_specs=pl.BlockSpec((tm, width), lambda i: (i, 0)),
        out_shape=jax.ShapeDtypeStruct((m, width), F32),
        compiler_params=_cparams(("parallel",)),
    )(z, z, z, mu.reshape(1, width))


def _even_mixer(u, b, t, n_ctx, w_in, mu, w0, w2, a0, a2, g2, k_k, k_a, r_k, gn_g, gn_b, lower,
                hg_norm_g, w_out):
    za, zb = matmul(u, w_in, splits=(RW_IN, HG_IN))
    za = token_shift(za, mu, t, n_ctx).reshape(b, t, RW_IN)
    zb = zb.reshape(b, t, HG_IN)
    yf, yb = rwkv7_scan(za, w0, w2, a0, a2, k_k, k_a, n_ctx)
    of, ob = hgrn2_scan(zb, lower, n_ctx)
    flat = lambda x: x.reshape(b * t, x.shape[-1])
    y = even_post(flat(za), flat(yf), flat(yb), flat(zb), flat(of), flat(ob), a0, a2, g2, k_a,
                  r_k.reshape(-1), gn_g, gn_b, hg_norm_g, w_out)
    return y


def _rope_tables(rows):
    t = jnp.arange(rows * GRID_W)
    row = (t // GRID_W).astype(F32)
    col = (t % GRID_W).astype(F32)
    half = MLA_ROPE // 2
    inv_freq = ROPE_BASE ** (-jnp.arange(0, half, 2, dtype=F32) / half)
    ang_r = row[:, None] * inv_freq
    ang_c = col[:, None] * inv_freq
    ang = jnp.concatenate([ang_r, ang_r, ang_c, ang_c], axis=-1)
    return jnp.cos(ang), jnp.sin(ang)


def _mla_weights(w_in, w_qb, w_kvb, w_out):
    hh, half = MLA_HEADS, MLA_HEADS // 2
    w_in_ext = jnp.concatenate([w_in, _rot_half(w_in[:, MLA_Q_RANK + MLA_KV_RANK:])], axis=1)
    kvb = w_kvb.reshape(MLA_KV_RANK, half, 2, 2, MLA_NOPE)
    kv128 = jnp.stack([kvb[:, :, 0], kvb[:, :, 1, ::-1]], axis=2).reshape(MLA_KV_RANK, hh, LANES)
    kv_rows = jnp.pad(kv128, ((0, 0), (0, 0), (0, MLA_SLOT - LANES))).reshape(MLA_KV_RANK, hh * MLA_SLOT)
    rope_rows = jnp.pad(jnp.eye(MLA_ROPE, dtype=F32), ((0, 0), (LANES, MLA_SLOT - LANES - MLA_ROPE)))
    w_kv_slots = jnp.concatenate([kv_rows, jnp.tile(rope_rows, (1, hh))], axis=0).astype(BF16)
    qb = w_qb.reshape(MLA_Q_RANK, hh, MLA_QK)
    nope = qb[..., :MLA_NOPE].reshape(MLA_Q_RANK, half, 2, MLA_NOPE)
    zeros = jnp.zeros_like(nope[:, :, 0])
    nope128 = jnp.stack([jnp.concatenate([nope[:, :, 0], zeros], -1),
                         jnp.concatenate([zeros, nope[:, :, 1]], -1)], axis=2).reshape(MLA_Q_RANK, hh, LANES)
    rope = qb[..., MLA_NOPE:]
    pad_rope = lambda x: jnp.pad(x, ((0, 0), (0, 0), (0, LANES - MLA_ROPE)))
    wq_slots = jnp.concatenate([nope128, pad_rope(rope)], -1).reshape(MLA_Q_RANK, hh * MLA_SLOT)
    wr_slots = pad_rope(_rot_half(rope)).reshape(MLA_Q_RANK, hh * LANES)
    w_out_perm = w_out.reshape(half, 2, MLA_V, -1)[:, ::-1].reshape(hh * MLA_V, -1)
    return w_in_ext, w_kv_slots, wq_slots.astype(BF16), wr_slots.astype(BF16), w_out_perm


def _mla_mixer(u, b, t, n_ctx, w_in, q_norm_g, w_qb, kv_norm_g, w_kvb, w_out, cos, sin):
    w_in_ext, w_kv_slots, wq_slots, wr_slots, w_out_perm = _mla_weights(w_in, w_qb, w_kvb, w_out)
    zq, z_kv = matmul(u, w_in_ext, splits=(MLA_Q_RANK, MLA_KV_RANK + 2 * MLA_ROPE))
    ones, zeros = jnp.ones((n_ctx, MLA_ROPE), F32), jnp.zeros((n_ctx, MLA_ROPE), F32)
    cs = jnp.concatenate([jnp.concatenate([ones, cos], axis=0), jnp.concatenate([zeros, sin], axis=0)], axis=1)
    kv_slots = mla_kv(z_kv, cs, kv_norm_g, w_kv_slots, t).reshape(b, t, MLA_HEADS * MLA_SLOT)
    zq = zq.reshape(b, t, MLA_Q_RANK)
    widen = lambda x, fill: jnp.pad(x, ((0, 0), (0, LANES - MLA_ROPE)), constant_values=fill)
    o_ctx = mla_attention(zq[:, :n_ctx], kv_slots, n_ctx, widen(ones, 1.0), widen(zeros, 0.0), q_norm_g,
                          wq_slots, wr_slots)
    o_lat = mla_attention(zq[:, n_ctx:], kv_slots, t, widen(cos, 1.0), widen(sin, 0.0), q_norm_g,
                          wq_slots, wr_slots)
    o = jnp.concatenate([o_ctx, o_lat], axis=1).reshape(b * t, MLA_HEADS * MLA_V)
    return matmul(o, w_out_perm)


def kernel(x, c, ctx, c_ctx, mod_w, mod_b, ln_g, ln_b, ev_w_in, rw_mu, rw_w0, rw_w2, rw_a0, rw_a2,
           rw_g2, rw_k_k, rw_k_a, rw_r_k, rw_gn_g, rw_gn_b, hg_lb, hg_norm_g, ev_w_out, od_w_in,
           mla_q_norm_g, mla_w_qb, mla_kv_norm_g, mla_w_kvb, od_w_out, moe_w_router, moe_b_router,
           moe_w_gu, moe_b_gu, moe_w_down, moe_b_down):
    b, n_lat, dm = x.shape
    n_ctx = ctx.shape[1]
    t = n_ctx + n_lat
    cos, sin = _rope_tables(n_lat // GRID_W)
    lb = jax.nn.softmax(hg_lb.astype(F32), axis=0)
    hg_lower = jnp.cumsum(lb, axis=0) - lb[0]
    c_act = c * jax.nn.sigmoid(c)
    cc_act = c_ctx * jax.nn.sigmoid(c_ctx)
    mod_in = jnp.concatenate([c_act, cc_act[None]], axis=0)
    pad = (-mod_in.shape[0]) % 8
    mod_in = jnp.pad(mod_in, ((0, pad), (0, 0)))
    mods = jnp.stack([matmul(mod_in, mod_w[layer]) + mod_b[layer] for layer in range(DEPTH)])
    mods = mods.reshape(DEPTH * mod_in.shape[0], 1, N_MOD * dm)
    nm = functools.partial(norm_mod, mods=mods, rows_per_seq=t, n_ctx=n_ctx, n_batch=b)
    h = jnp.concatenate([ctx, x], axis=1).reshape(b * t, dm)
    u16 = nm(h, None, mod=(0, 0, 1), outs=('u16',))
    for layer in range(DEPTH):
        j = layer // 2
        if layer % 2 == 0:
            y = _even_mixer(u16, b, t, n_ctx, ev_w_in[j], rw_mu[j], rw_w0[j], rw_w2[j], rw_a0[j],
                            rw_a2[j], rw_g2[j], rw_k_k[j], rw_k_a[j], rw_r_k[j], rw_gn_g[j], rw_gn_b[j],
                            hg_lower[j], hg_norm_g[j], ev_w_out[j])
        else:
            y = _mla_mixer(u16, b, t, n_ctx, od_w_in[j], mla_q_norm_g[j], mla_w_qb[j],
                           mla_kv_norm_g[j], mla_w_kvb[j], od_w_out[j], cos, sin)
        h, u32, u16 = nm(h, y, gate=(layer, 2), ln=(ln_g[layer, 0], ln_b[layer, 0]), mod=(layer, 3, 4),
                         outs=('h', 'u32', 'u16'))
        f, fg = moe(u32, u16, layer, moe_w_router[layer], moe_b_router[layer], moe_w_gu, moe_b_gu,
                    moe_w_down, moe_b_down)
        if layer + 1 < DEPTH:
            h, u16 = nm(h, f, gate=(layer, 5), ln=(ln_g[layer, 1], ln_b[layer, 1]),
                        mod=(layer + 1, 0, 1), outs=('h', 'u16'), slot_gates=fg)
        else:
            h = nm(h, f, gate=(layer, 5), ln=(ln_g[layer, 1], ln_b[layer, 1]), outs=('h',),
                   slot_gates=fg)
    return h.reshape(b, t, dm)[:, n_ctx:]
```

```python
import functools

import jax
import jax.numpy as jnp
from jax import lax
from jax.experimental import pallas as pl
from jax.experimental.pallas import tpu as pltpu

F32 = jnp.float32
BF16 = jnp.bfloat16
HIGHEST = lax.Precision.HIGHEST

DEPTH = 4
GRID_W = 64
N_MOD = 6

RW_HEADS = 8
RW_HEAD_DIM = 64
RW_DIM = RW_HEADS * RW_HEAD_DIM
RW_DECAY_LORA = 64
RW_ICLR_LORA = 64
RW_GATE_LORA = 128
RW_GN_EPS = 64e-5
RW_IN = 3 * RW_DIM + 2 * RW_DECAY_LORA + 2 * RW_ICLR_LORA + RW_GATE_LORA
RW_LORA_OFF = 3 * RW_DIM

HG_HEADS = 4
HG_KEY_DIM = 128
HG_DIM = HG_HEADS * HG_KEY_DIM
HG_VDIM = HG_DIM
HG_IN = 3 * HG_DIM + 2 * HG_VDIM
HG_F_FLOOR = 1e-30

MLA_HEADS = 16
MLA_NOPE = 64
MLA_ROPE = 32
MLA_V = 64
MLA_Q_RANK = 256
MLA_KV_RANK = 128
MLA_QK = MLA_NOPE + MLA_ROPE
MLA_SCALE = MLA_QK ** -0.5
ROPE_BASE = 10000.0

N_EXPERTS = 32
TOP_K = 4
EXPERT_DIM = 1024
SWIGLU_LIMIT = 7.0
SWIGLU_ALPHA = 1.702

DEEPNORM_ALPHA = (2 * DEPTH) ** 0.25
LN_EPS = 1e-5
RMS_EPS = 1e-6

LANES = 128
SCAN_CHUNK = 64
HG_SUB = 16
MOE_TILE = 512
RANK_BLOCK = 256
VMEM_LIMIT = 56 * 1024 * 1024


def _cparams(sem):
    return pltpu.CompilerParams(dimension_semantics=sem, vmem_limit_bytes=VMEM_LIMIT)


def _row_tile(m, tm):
    if m <= tm:
        return m
    while m % tm:
        tm //= 2
    assert tm >= 8
    return tm


def _mm_kernel(x_ref, w_ref, *o_refs):
    acc = jnp.dot(x_ref[...].astype(BF16), w_ref[...], preferred_element_type=F32)
    off = 0
    for o_ref in o_refs:
        width = o_ref.shape[1]
        o_ref[...] = acc[:, off:off + width].astype(o_ref.dtype)
        off += width


def matmul(x, w, splits=None, out_dtype=F32, tm=512):
    m, k = x.shape
    n = w.shape[1]
    tm = _row_tile(m, tm)
    widths = (n,) if splits is None else splits
    assert sum(widths) == n
    outs = pl.pallas_call(
        _mm_kernel,
        grid=(m // tm,),
        in_specs=[pl.BlockSpec((tm, k), lambda i: (i, 0)),
                  pl.BlockSpec((k, n), lambda i: (0, 0))],
        out_specs=[pl.BlockSpec((tm, wd), lambda i: (i, 0)) for wd in widths],
        out_shape=[jax.ShapeDtypeStruct((m, wd), out_dtype) for wd in widths],
        compiler_params=_cparams(("parallel",)),
    )(x, w.astype(BF16))
    return outs[0] if splits is None else outs


def _route_top4(u, w, b):
    x = jnp.dot(u, w, preferred_element_type=F32, precision=HIGHEST) + b
    tm, ne = x.shape
    lane = lax.broadcasted_iota(jnp.int32, (tm, ne), 1).astype(F32)
    slot = lax.broadcasted_iota(jnp.int32, (tm, LANES), 1)
    idx = jnp.zeros((tm, LANES), F32)
    tops = []
    for k in range(TOP_K):
        mx = jnp.max(x, axis=-1, keepdims=True)
        ix = jnp.min(jnp.where(x == mx, lane, float(ne)), axis=-1, keepdims=True)
        tops.append(mx)
        idx = jnp.where(slot == k, ix, idx)
        x = jnp.where(lane == ix, -jnp.inf, x)
    e = [jnp.exp(v - tops[0]) for v in tops]
    inv = 1.0 / sum(e)
    gates = jnp.zeros((tm, LANES), F32)
    for k in range(TOP_K):
        gates = jnp.where(slot == k, e[k] * inv, gates)
    return idx.astype(jnp.int32), gates


def _bwd_chunk(c, n_ctx_chunks, n_chunks):
    return jnp.where(c < n_ctx_chunks, n_ctx_chunks - 1 - c, n_chunks - 1 - (c - n_ctx_chunks))


def _dot(a, b):
    return jnp.dot(a.astype(BF16), b.astype(BF16), preferred_element_type=F32)


def _dot_nt(a, b):
    return lax.dot_general(a.astype(BF16), b.astype(BF16), (((1,), (1,)), ((), ())),
                           preferred_element_type=F32)


def _dot_tn(a, b):
    return lax.dot_general(a.astype(BF16), b.astype(BF16), (((0,), (0,)), ((), ())),
                           preferred_element_type=F32)


def _dot_f32(a, b):
    return jnp.dot(a, b, preferred_element_type=F32, precision=HIGHEST)


def _seg_sum(x, ones_bd):
    hi = x.astype(BF16)
    lo = (x - hi.astype(F32)).astype(BF16)
    return (jnp.dot(hi, ones_bd, preferred_element_type=F32)
            + jnp.dot(lo, ones_bd, preferred_element_type=F32))


def _block_diag_ones(width, block):
    i = jnp.arange(width) // block
    return (i[:, None] == i[None, :]).astype(BF16)


def _hgrn_kernel(zf_ref, zb_ref, lo_ref, bd_ref, of_ref, ob_ref, st_ref):
    c = pl.program_id(1)
    n, sub = SCAN_CHUNK, HG_SUB
    nb = n // sub

    @pl.when(c == 0)
    def _():
        st_ref[...] = jnp.zeros_like(st_ref)

    row = lax.broadcasted_iota(jnp.int32, (n, n), 0)
    col = lax.broadcasted_iota(jnp.int32, (n, n), 1)
    colb = lax.broadcasted_iota(jnp.int32, (sub, LANES), 1)
    rown = lax.broadcasted_iota(jnp.int32, (sub, n), 0)
    coln = lax.broadcasted_iota(jnp.int32, (sub, n), 1)
    bd2 = bd_ref[...]

    dirs = []
    for d, z_ref in ((0, zf_ref), (1, zb_ref)):
        z = z_ref[0]
        lower = lo_ref[d]
        qraw = z[:, :HG_DIM]
        sig = jax.nn.sigmoid(z[:, HG_DIM * (1 + d):HG_DIM * (2 + d)])
        v = z[:, 3 * HG_DIM:3 * HG_DIM + HG_VDIM]
        f = lower + (1.0 - lower) * sig
        g = jnp.log(jnp.maximum(f, HG_F_FLOOR))
        kin = (1.0 - lower) * (1.0 - sig)
        q = qraw * jax.nn.sigmoid(qraw)
        lag = (row - col) if d == 0 else (col - row)
        cum = _dot_f32((lag >= 0).astype(F32), g)
        cum_ex = cum - g
        tot = jnp.sum(g, axis=0, keepdims=True)
        blocks = []
        for j in range(nb):
            lo, hi = j * sub, (j + 1) * sub
            q_j, cum_j = q[lo:hi], cum[lo:hi]
            ref_j = cum_ex[lo:lo + 1] if d == 0 else cum_ex[hi - 1:hi]
            qt = q_j * jnp.exp(cum_j - ref_j)
            kt = kin * jnp.exp(jnp.minimum(ref_j - cum, 0.0))
            pair = jnp.concatenate(
                [q_j * kin[s:s + 1] * jnp.exp(jnp.minimum(cum_j - cum[s:s + 1], 0.0))
                 for s in range(lo, hi)], axis=0)
            blocks.append((qt, kt, pair))
        dirs.append(dict(d=d, v=v, qe=q * jnp.exp(cum), kdec=kin * jnp.exp(tot - cum),
                         etot=jnp.exp(tot), blocks=blocks))

    for dd, o_ref in zip(dirs, (of_ref, ob_ref)):
        d = dd['d']
        for j, (qt, kt, pair) in enumerate(dd['blocks']):
            lo, hi = j * sub, (j + 1) * sub
            sums = [_dot(pair[:, 2 * LANES * hp:2 * LANES * (hp + 1)], bd2) for hp in range(HG_HEADS // 2)]
            for h in range(HG_HEADS):
                hs = slice(LANES * h, LANES * (h + 1))
                rs = sums[h // 2][:, LANES * (h % 2):LANES * (h % 2 + 1)]
                diag = jnp.zeros((sub, LANES), F32)
                for s in range(sub):
                    diag = jnp.where(colb == lo + s, rs[s * sub:(s + 1) * sub], diag)
                off = _dot_nt(qt[:, hs], kt[:, hs])
                if d == 0:
                    earlier, ordered = coln < lo, rown + lo >= coln
                else:
                    earlier, ordered = coln >= hi, rown + lo <= coln
                sc = jnp.where(earlier, off, jnp.where(ordered, diag[:, :n], 0.0))
                dd.setdefault(('out', h), []).append(_dot(sc, dd['v'][:, hs]))

    for dd, o_ref in zip(dirs, (of_ref, ob_ref)):
        d = dd['d']
        for h in range(HG_HEADS):
            hs = slice(LANES * h, LANES * (h + 1))
            st = st_ref[d * HG_HEADS + h]
            out = jnp.concatenate(dd[('out', h)], axis=0) + _dot_nt(dd['qe'][:, hs], st)
            o_ref[0, :, hs] = out
            st_ref[d * HG_HEADS + h] = (st * dd['etot'][:, hs]
                                        + _dot_tn(dd['v'][:, hs], dd['kdec'][:, hs]))


def hgrn2_scan(zb, lower, n_ctx):
    b, t, _ = zb.shape
    n = SCAN_CHUNK
    nc, ncc = t // n, n_ctx // n
    bwd = functools.partial(_bwd_chunk, n_ctx_chunks=ncc, n_chunks=nc)
    const = lambda shape: pl.BlockSpec(shape, lambda bi, c: (0,) * len(shape))
    return pl.pallas_call(
        _hgrn_kernel,
        grid=(b, nc),
        in_specs=[pl.BlockSpec((1, n, HG_IN), lambda bi, c: (bi, c, 0)),
                  pl.BlockSpec((1, n, HG_IN), lambda bi, c: (bi, bwd(c), 0)),
                  const((2, 1, HG_DIM)), const((2 * LANES, 2 * LANES))],
        out_specs=[pl.BlockSpec((1, n, HG_VDIM), lambda bi, c: (bi, c, 0)),
                   pl.BlockSpec((1, n, HG_VDIM), lambda bi, c: (bi, bwd(c), 0))],
        out_shape=[jax.ShapeDtypeStruct((b, t, HG_VDIM), F32)] * 2,
        scratch_shapes=[pltpu.VMEM((2 * HG_HEADS, HG_KEY_DIM, HG_KEY_DIM), F32)],
        compiler_params=_cparams(("parallel", "arbitrary")),
    )(zb, zb, lower.reshape(2, 1, HG_DIM), _block_diag_ones(2 * LANES, LANES))


def _rwkv_iclr(z, d, a0_ref, a2_ref):
    ad = z[:, RW_LORA_OFF + LANES:RW_LORA_OFF + 2 * LANES]
    return jax.nn.sigmoid(a0_ref[d] + _dot(ad, a2_ref[d]))


def _rwkv_scan_kernel(zf_ref, zb_ref, w0_ref, w2_ref, a0_ref, a2_ref, kk_ref, ka_ref, bd_ref,
                      yf_ref, yb_ref, st_ref):
    c = pl.program_id(1)
    n, cdim = SCAN_CHUNK, RW_DIM
    npair = cdim // LANES

    @pl.when(c == 0)
    def _():
        st_ref[...] = jnp.zeros_like(st_ref)

    row = lax.broadcasted_iota(jnp.int32, (n, n), 0)
    col = lax.broadcasted_iota(jnp.int32, (n, n), 1)
    row4 = lax.broadcasted_iota(jnp.int32, (2 * n, 4 * n), 0) % n
    col4 = lax.broadcasted_iota(jnp.int32, (2 * n, 4 * n), 1) % n
    lane = lax.broadcasted_iota(jnp.int32, (1, LANES), 1)
    half = [lane < RW_HEAD_DIM, lane >= RW_HEAD_DIM]

    def stack2(x):
        return jnp.concatenate([jnp.where(half[0], x, 0.0), jnp.where(half[1], x, 0.0)], axis=0)

    r2 = lax.broadcasted_iota(jnp.int32, (LANES, LANES), 0) < RW_HEAD_DIM
    c2 = lax.broadcasted_iota(jnp.int32, (LANES, LANES), 1) < RW_HEAD_DIM
    same_head = r2 == c2

    chains = []
    for d, z_ref in ((0, zf_ref), (1, zb_ref)):
        z = z_ref[0]
        r, k, v = z[:, :cdim], z[:, cdim:2 * cdim], z[:, 2 * cdim:3 * cdim]
        wd = z[:, RW_LORA_OFF:RW_LORA_OFF + LANES]
        w_pre = w0_ref[d] + _dot(jnp.tanh(wd), w2_ref[d])
        lw = -jnp.exp(-0.5) * jax.nn.sigmoid(w_pre)
        a = _rwkv_iclr(z, d, a0_ref, a2_ref)
        kk = k * kk_ref[...]
        kk = kk * lax.rsqrt(jnp.maximum(_seg_sum(kk * kk, bd_ref[...]), 1e-24))
        kd = k * (1.0 + (a - 1.0) * ka_ref[...])
        bb = kk * a
        lag = (row - col) if d == 0 else (col - row)
        lag4 = (row4 - col4) if d == 0 else (col4 - row4)
        incl4, strict4 = lag4 >= 0, lag4 > 0
        cum = _dot_f32((lag >= 0).astype(F32), lw)
        tot = jnp.sum(lw, axis=0, keepdims=True)
        g_inv = jnp.exp(-cum)
        g_tail = jnp.exp(tot - cum)
        rt = r * jnp.exp(cum)
        at = -kk * jnp.exp(cum - lw)
        etot = jnp.exp(tot)
        bt, kt = bb * g_inv, kd * g_inv
        tail = jnp.concatenate([bb * g_tail, kd * g_tail], axis=0)
        for p in range(npair):
            ps = slice(LANES * p, LANES * (p + 1))
            chains.append(dict(d=d, p=p, ps=ps, incl4=incl4, strict4=strict4, tail=tail[:, ps],
                               at=stack2(at[:, ps]), rt=rt[:, ps], bt=bt[:, ps], kt=kt[:, ps],
                               v=v[:, ps], etot=etot[:, ps]))

    for ch in chains:
        left = jnp.concatenate([ch['at'], stack2(ch['rt'])], axis=0)
        right = jnp.concatenate([stack2(ch['bt']), stack2(ch['kt'])], axis=0)
        prod = _dot_nt(left, right)
        ch['top'] = jnp.where(ch['strict4'], prod[:2 * n], 0.0)
        ch['bot'] = jnp.where(ch['incl4'], prod[2 * n:], 0.0)
    for ch in chains:
        ch['vs'] = stack2(ch['v'])
        ch['x'] = jnp.concatenate([ch['at'], _dot(ch['top'][:, 2 * n:], ch['vs'])], axis=1)
        ch['a'] = ch['top'][:, :2 * n]
    steps = n.bit_length() - 1
    for i in range(steps):
        for ch in chains:
            ch['x'] = ch['x'] + _dot(ch['a'], ch['x'])
            if i + 1 < steps:
                ch['a'] = _dot(ch['a'], ch['a'])
    for ch in chains:
        x = ch['x'][:n] + ch['x'][n:]
        st = st_ref[ch['d'] * npair + ch['p']]
        ws = _dot_nt(jnp.concatenate([x[:, :LANES], ch['rt']], axis=0), st)
        u = ws[:n] + x[:, LANES:]
        ys = _dot(ch['bot'], jnp.concatenate([stack2(u), ch['vs']], axis=0))
        y_ref = yf_ref if ch['d'] == 0 else yb_ref
        y_ref[0, :, ch['ps']] = ws[n:] + ys[:n] + ys[n:]
        delta = _dot_tn(jnp.concatenate([u, ch['v']], axis=0), ch['tail'])
        st_ref[ch['d'] * npair + ch['p']] = st * ch['etot'] + jnp.where(same_head, delta, 0.0)


def _pad_lora(w, d):
    return jnp.concatenate([w[d] if i == d else jnp.zeros_like(w[i]) for i in range(2)], axis=0)


def rwkv7_scan(za, w0, w2, a0, a2, k_k, k_a, n_ctx):
    b, t, _ = za.shape
    n, cdim = SCAN_CHUNK, RW_DIM
    nc, ncc = t // n, n_ctx // n
    bwd = functools.partial(_bwd_chunk, n_ctx_chunks=ncc, n_chunks=nc)
    const = lambda shape: pl.BlockSpec(shape, lambda bi, c: (0,) * len(shape))
    w2p = jnp.stack([_pad_lora(w2, d) for d in range(2)]).astype(BF16)
    a2p = jnp.stack([_pad_lora(a2, d) for d in range(2)]).astype(BF16)
    return pl.pallas_call(
        _rwkv_scan_kernel,
        grid=(b, nc),
        in_specs=[pl.BlockSpec((1, n, RW_IN), lambda bi, c: (bi, c, 0)),
                  pl.BlockSpec((1, n, RW_IN), lambda bi, c: (bi, bwd(c), 0)),
                  const((2, 1, cdim)), const((2, LANES, cdim)), const((2, 1, cdim)),
                  const((2, LANES, cdim)), const((1, cdim)), const((1, cdim)), const((cdim, cdim))],
        out_specs=[pl.BlockSpec((1, n, cdim), lambda bi, c: (bi, c, 0)),
                   pl.BlockSpec((1, n, cdim), lambda bi, c: (bi, bwd(c), 0))],
        out_shape=[jax.ShapeDtypeStruct((b, t, cdim), F32)] * 2,
        scratch_shapes=[pltpu.VMEM((2 * cdim // LANES, LANES, LANES), F32)],
        compiler_params=_cparams(("parallel", "arbitrary")),
    )(za, za, w0.reshape(2, 1, cdim), w2p, a0.reshape(2, 1, cdim), a2p, k_k.reshape(1, cdim),
      k_a.reshape(1, cdim), _block_diag_ones(cdim, RW_HEAD_DIM))


def _even_post_kernel(za_ref, yf_ref, yb_ref, zg_ref, of_ref, ob_ref, a0_ref, a2_ref, g2_ref,
                      ka_ref, rk_ref, gng_ref, gnb_ref, hgg_ref, bd64_ref, bd128_ref, wout_ref,
                      o_ref):
    cdim = RW_DIM
    z = za_ref[...]
    r, k, v = z[:, :cdim], z[:, cdim:2 * cdim], z[:, 2 * cdim:3 * cdim]
    a_sum = _rwkv_iclr(z, 0, a0_ref, a2_ref) + _rwkv_iclr(z, 1, a0_ref, a2_ref)
    k_sum = k * (2.0 + (a_sum - 2.0) * ka_ref[...])
    gate = _dot(jax.nn.sigmoid(z[:, RW_LORA_OFF + 2 * LANES:]), g2_ref[...])
    bd64 = bd64_ref[...]
    y = yf_ref[...] + yb_ref[...]
    mean = _seg_sum(y, bd64) * (1.0 / RW_HEAD_DIM)
    cen = y - mean
    var = _seg_sum(cen * cen, bd64) * (1.0 / RW_HEAD_DIM)
    y = cen * lax.rsqrt(var + RW_GN_EPS) * gng_ref[...] + gnb_ref[...]
    bonus = _seg_sum(r * k_sum * rk_ref[...], bd64) * v
    y_rw = (y + bonus) * gate

    o = of_ref[...] + ob_ref[...]
    ms = _seg_sum(o * o, bd128_ref[...]) * (1.0 / HG_KEY_DIM)
    g_hg = zg_ref[...]
    y_hg = o * lax.rsqrt(ms + RMS_EPS) * hgg_ref[...] * (g_hg * jax.nn.sigmoid(g_hg))

    y_all = jnp.concatenate([y_rw, y_hg], axis=1).astype(BF16)
    o_ref[...] = jnp.dot(y_all, wout_ref[...], preferred_element_type=F32)


def even_post(za, yf, yb, zb, of, ob, a0, a2, g2, k_a, r_k, gn_g, gn_b, hg_norm_g, w_out, tm=256):
    m = za.shape[0]
    cdim, dm = RW_DIM, w_out.shape[1]
    tm = _row_tile(m, tm)
    rows = lambda width, blk=0: pl.BlockSpec((tm, width), lambda i: (i, blk))
    const = lambda shape: pl.BlockSpec(shape, lambda i: (0,) * len(shape))
    a2p = jnp.stack([_pad_lora(a2, d) for d in range(2)]).astype(BF16)
    vec = lambda x: x.reshape(1, cdim)
    return pl.pallas_call(
        _even_post_kernel,
        grid=(m // tm,),
        in_specs=[rows(RW_IN), rows(cdim), rows(cdim), rows(HG_VDIM, (HG_IN - HG_VDIM) // HG_VDIM),
                  rows(HG_VDIM), rows(HG_VDIM),
                  const((2, 1, cdim)), const((2, LANES, cdim)), const((RW_GATE_LORA, cdim)),
                  const((1, cdim)), const((1, cdim)), const((1, cdim)), const((1, cdim)),
                  const((1, HG_VDIM)), const((cdim, cdim)), const((HG_VDIM, HG_VDIM)),
                  const((cdim + HG_VDIM, dm))],
        out_specs=rows(dm),
        out_shape=jax.ShapeDtypeStruct((m, dm), F32),
        compiler_params=_cparams(("parallel",)),
    )(za, yf, yb, zb, of, ob, a0.reshape(2, 1, cdim), a2p, g2.astype(BF16), vec(k_a), vec(r_k),
      vec(gn_g), vec(gn_b), hg_norm_g.reshape(1, HG_VDIM), _block_diag_ones(cdim, RW_HEAD_DIM),
      _block_diag_ones(HG_VDIM, HG_KEY_DIM), w_out.astype(BF16))


MLA_SLOT = 2 * LANES
MLA_GROUP = 4


def _rot_half(w):
    r1, r2, c1, c2 = jnp.split(w, 4, axis=-1)
    return jnp.concatenate([-r2, r1, -c2, c1], axis=-1)


def _mla_kv_kernel(z_ref, cs_ref, g_ref, w_ref, o_ref):
    z = z_ref[...]
    kva = z[:, :MLA_KV_RANK]
    kva = kva * lax.rsqrt(jnp.mean(kva * kva, axis=-1, keepdims=True) + RMS_EPS) * g_ref[...]
    cs = cs_ref[...]
    kpe = (z[:, MLA_KV_RANK:MLA_KV_RANK + MLA_ROPE] * cs[:, :MLA_ROPE]
           + z[:, MLA_KV_RANK + MLA_ROPE:] * cs[:, MLA_ROPE:])
    x = jnp.concatenate([kva, kpe], axis=1).astype(BF16)
    o_ref[...] = jnp.dot(x, w_ref[...], preferred_element_type=F32).astype(o_ref.dtype)


def mla_kv(z_kv, cs, kv_norm_g, w_kv_slots, rows_per_seq, tm=256):
    m, width = z_kv.shape
    tm = _row_tile(rows_per_seq, tm)
    per_seq = rows_per_seq // tm
    n_out = w_kv_slots.shape[1]
    return pl.pallas_call(
        _mla_kv_kernel,
        grid=(m // tm,),
        in_specs=[pl.BlockSpec((tm, width), lambda i: (i, 0)),
                  pl.BlockSpec((tm, 2 * MLA_ROPE), lambda i: (i % per_seq, 0)),
                  pl.BlockSpec((1, MLA_KV_RANK), lambda i: (0, 0)),
                  pl.BlockSpec(w_kv_slots.shape, lambda i: (0, 0))],
        out_specs=pl.BlockSpec((tm, n_out), lambda i: (i, 0)),
        out_shape=jax.ShapeDtypeStruct((m, n_out), BF16),
        compiler_params=_cparams(("parallel",)),
    )(z_kv, cs, kv_norm_g.reshape(1, MLA_KV_RANK), w_kv_slots)


def _mla_attn_kernel(zq_ref, kv_ref, cos_ref, sin_ref, g_ref, wq_ref, wr_ref, o_ref):
    zq = zq_ref[0]
    zn = (zq * lax.rsqrt(jnp.mean(zq * zq, axis=-1, keepdims=True) + RMS_EPS) * g_ref[...]).astype(BF16)
    a = jnp.dot(zn, wq_ref[...], preferred_element_type=F32)
    ar = jnp.dot(zn, wr_ref[...], preferred_element_type=F32)
    cos, sin = cos_ref[...], sin_ref[...]
    low = lax.broadcasted_iota(jnp.int32, (1, LANES), 1) < MLA_V
    blocks = []
    for pair in range(MLA_GROUP // 2):
        res = []
        for j in range(2):
            h = 2 * pair + j
            q_nope = a[:, MLA_SLOT * h:MLA_SLOT * h + LANES]
            q_rope = (a[:, MLA_SLOT * h + LANES:MLA_SLOT * (h + 1)] * cos
                      + ar[:, LANES * h:LANES * (h + 1)] * sin)
            qf = (jnp.concatenate([q_nope, q_rope], axis=1) * MLA_SCALE).astype(BF16)
            s = lax.dot_general(qf, kv_ref[0, :, MLA_SLOT * h:MLA_SLOT * (h + 1)],
                                (((1,), (1,)), ((), ())), preferred_element_type=F32)
            p = jnp.exp(s - jnp.max(s, axis=-1, keepdims=True))
            l = jnp.sum(p, axis=-1, keepdims=True)
            res.append(jnp.dot(p.astype(BF16), kv_ref[0, :, MLA_SLOT * h:MLA_SLOT * h + LANES],
                               preferred_element_type=F32) / l)
        blocks.append(jnp.where(low, res[1], res[0]))
    o_ref[0] = jnp.concatenate(blocks, axis=1).astype(o_ref.dtype)


def mla_attention(zq, kv_slots, n_keys, cos, sin, q_norm_g, wq_slots, wr_slots, tq=512):
    b, nq, _ = zq.shape
    tq = _row_tile(nq, tq)
    gw = MLA_GROUP * MLA_SLOT
    return pl.pallas_call(
        _mla_attn_kernel,
        grid=(b, MLA_HEADS // MLA_GROUP, nq // tq),
        in_specs=[pl.BlockSpec((1, tq, MLA_Q_RANK), lambda bi, g, qi: (bi, qi, 0)),
                  pl.BlockSpec((1, n_keys, gw), lambda bi, g, qi: (bi, 0, g)),
                  pl.BlockSpec((tq, LANES), lambda bi, g, qi: (qi, 0)),
                  pl.BlockSpec((tq, LANES), lambda bi, g, qi: (qi, 0)),
                  pl.BlockSpec((1, MLA_Q_RANK), lambda bi, g, qi: (0, 0)),
                  pl.BlockSpec((MLA_Q_RANK, gw), lambda bi, g, qi: (0, g)),
                  pl.BlockSpec((MLA_Q_RANK, gw // 2), lambda bi, g, qi: (0, g))],
        out_specs=pl.BlockSpec((1, tq, MLA_GROUP * MLA_V), lambda bi, g, qi: (bi, qi, g)),
        out_shape=jax.ShapeDtypeStruct((b, nq, MLA_HEADS * MLA_V), BF16),
        compiler_params=_cparams(("parallel", "parallel", "parallel")),
    )(zq, kv_slots, cos, sin, q_norm_g.reshape(1, MLA_Q_RANK), wq_slots, wr_slots)


def _moe_kernel(te_ref, nt_ref, x_ref, wgu_ref, bgu_ref, wd_ref, bd_ref, o_ref, wgu_bf, wd_bf):
    i = pl.program_id(0)
    live = i < nt_ref[0]
    new_expert = jnp.logical_or(i == 0, te_ref[i] != te_ref[jnp.maximum(i - 1, 0)])

    @pl.when(jnp.logical_and(live, new_expert))
    def _():
        wgu_bf[...] = wgu_ref[0, 0].astype(BF16)
        wd_bf[...] = wd_ref[0, 0].astype(BF16)

    @pl.when(live)
    def _():
        h = jnp.dot(x_ref[...], wgu_bf[...], preferred_element_type=F32) + bgu_ref[0, 0]
        glu = jnp.minimum(h[:, :EXPERT_DIM], SWIGLU_LIMIT)
        lin = jnp.clip(h[:, EXPERT_DIM:], -SWIGLU_LIMIT, SWIGLU_LIMIT)
        act = glu * jax.nn.sigmoid(SWIGLU_ALPHA * glu) * (lin + 1.0)
        y = jnp.dot(act.astype(BF16), wd_bf[...], preferred_element_type=F32) + bd_ref[0, 0]
        o_ref[...] = y.astype(o_ref.dtype)

    @pl.when(i >= nt_ref[0])
    def _():
        o_ref[...] = jnp.zeros_like(o_ref)


def moe_grouped(x_sorted, tile_expert, n_tiles_used, layer, w_gu, b_gu, w_down, b_down):
    p, dm = x_sorted.shape
    tm = MOE_TILE
    nl, e, _, f2 = w_gu.shape
    grid_spec = pltpu.PrefetchScalarGridSpec(
        num_scalar_prefetch=2,
        grid=(p // tm,),
        in_specs=[
            pl.BlockSpec((tm, dm), lambda i, te, nt: (i, 0)),
            pl.BlockSpec((1, 1, dm, f2), lambda i, te, nt: (layer, te[i], 0, 0)),
            pl.BlockSpec((1, 1, 1, f2), lambda i, te, nt: (layer, te[i], 0, 0)),
            pl.BlockSpec((1, 1, f2 // 2, dm), lambda i, te, nt: (layer, te[i], 0, 0)),
            pl.BlockSpec((1, 1, 1, dm), lambda i, te, nt: (layer, te[i], 0, 0)),
        ],
        out_specs=pl.BlockSpec((tm, dm), lambda i, te, nt: (i, 0)),
        scratch_shapes=[pltpu.VMEM((dm, f2), BF16), pltpu.VMEM((f2 // 2, dm), BF16)],
    )
    return pl.pallas_call(
        _moe_kernel,
        grid_spec=grid_spec,
        out_shape=jax.ShapeDtypeStruct((p, dm), F32),
        compiler_params=_cparams(("arbitrary",)),
    )(tile_expert, n_tiles_used, x_sorted, w_gu, b_gu.reshape(nl, e, 1, f2), w_down,
      b_down.reshape(nl, e, 1, dm))


def moe(idx_lanes, gate_lanes, u16, layer, w_gu, b_gu, w_down, b_down):
    n, dm = u16.shape
    tm = MOE_TILE
    top_idx = idx_lanes[:, :TOP_K]
    e_flat = top_idx.T.reshape(-1).astype(jnp.int32)
    npair = n * TOP_K
    rb = _row_tile(npair, RANK_BLOCK)
    onehot = (e_flat[:, None] == jnp.arange(N_EXPERTS, dtype=jnp.int32)[None, :])
    oh3 = onehot.astype(BF16).reshape(npair // rb, rb, N_EXPERTS)
    earlier = (jnp.arange(rb)[:, None] > jnp.arange(rb)[None, :]).astype(BF16)
    within = jnp.einsum('ij,bje->bie', earlier, oh3, preferred_element_type=F32)
    blk_tot = jnp.sum(oh3.astype(F32), axis=1)
    blk_off = jnp.cumsum(blk_tot, axis=0) - blk_tot
    counts = jnp.sum(blk_tot, axis=0).astype(jnp.int32)
    padded = ((counts + tm - 1) // tm) * tm
    ends_p = jnp.cumsum(padded)
    starts_p = ends_p - padded
    starts = jnp.cumsum(counts) - counts
    rank = within + blk_off[:, None, :] + starts_p.astype(F32)[None, None, :]
    pos = jnp.sum(oh3.astype(F32) * rank, axis=-1).astype(jnp.int32).reshape(npair)
    p_rows = npair + N_EXPERTS * tm
    n_tiles = p_rows // tm
    tile_start = jnp.arange(n_tiles, dtype=jnp.int32) * tm
    tile_expert = jnp.minimum(jnp.sum((ends_p[None, :] <= tile_start[:, None]).astype(jnp.int32), axis=1),
                              N_EXPERTS - 1)
    n_used = (ends_p[-1] // tm).astype(jnp.int32).reshape(1)
    _, sorted_tok = lax.sort_key_val(e_flat, jnp.arange(npair, dtype=jnp.int32) % n)
    onehot_t = (tile_expert[:, None] == jnp.arange(N_EXPERTS, dtype=jnp.int32)[None, :]).astype(jnp.int32)
    shift = jnp.sum(onehot_t * (starts - starts_p)[None, :], axis=1)
    last = jnp.sum(onehot_t * (starts + counts)[None, :], axis=1)
    rows = jnp.arange(p_rows, dtype=jnp.int32).reshape(n_tiles, tm)
    src = rows + shift[:, None]
    valid = src < last[:, None]
    src_tok = jnp.where(valid, sorted_tok[jnp.where(valid, src, rows % npair).reshape(-1)].reshape(n_tiles, tm),
                        rows % n).reshape(-1)
    x_sorted = u16[src_tok]
    y_sorted = moe_grouped(x_sorted, tile_expert, n_used, layer, w_gu, b_gu, w_down, b_down)
    return y_sorted[pos].reshape(TOP_K, n, dm), gate_lanes


def _norm_mod_kernel(*refs, dm, gate, mod, has_norm, combine, route, outs):
    refs = list(refs)
    h = refs.pop(0)[...]
    if has_norm:
        if combine:
            y_ref, slot_gate = refs.pop(0), refs.pop(0)[...]
            y = sum(y_ref[k] * slot_gate[:, k:k + 1] for k in range(TOP_K))
        else:
            y = refs.pop(0)[...]
        mg = refs.pop(0)
        lng, lnb = refs.pop(0)[...], refs.pop(0)[...]
        x = DEEPNORM_ALPHA * h + mg[0, :, gate * dm:(gate + 1) * dm] * y
        mu = jnp.mean(x, axis=-1, keepdims=True)
        cen = x - mu
        var = jnp.mean(cen * cen, axis=-1, keepdims=True)
        h = cen * lax.rsqrt(var + LN_EPS) * lng + lnb
    if mod is not None:
        mm = refs.pop(0)
        shift, scale = mod
        u = h * (1.0 + mm[0, :, scale * dm:(scale + 1) * dm]) + mm[0, :, shift * dm:(shift + 1) * dm]
    vals = {'h': h}
    if mod is not None:
        vals['u16'] = u
    if route:
        wr, br = refs.pop(0)[...], refs.pop(0)[...]
        vals['idx'], vals['gates'] = _route_top4(u, wr, br)
    for kind, o_ref in zip(outs, refs):
        o_ref[...] = vals[kind].astype(o_ref.dtype)


def norm_mod(h, y, mods, rows_per_seq, n_ctx, n_batch, gate=None, ln=None, mod=None, outs=('h',),
             slot_gates=None, router=None, tm=256):
    m, dm = h.shape
    tm = _row_tile(n_ctx, tm)
    per_seq, ctx_tiles = rows_per_seq // tm, n_ctx // tm
    rows = mods.shape[0] // DEPTH

    def mod_row(layer):
        return lambda i: (layer * rows + jnp.where(i % per_seq < ctx_tiles, n_batch, i // per_seq), 0, 0)

    tile = pl.BlockSpec((tm, dm), lambda i: (i, 0))
    vec = pl.BlockSpec((1, dm), lambda i: (0, 0))
    args, specs = [h], [tile]
    if gate is not None:
        if slot_gates is None:
            args.append(y)
            specs.append(tile)
        else:
            args += [y, slot_gates]
            specs += [pl.BlockSpec((TOP_K, tm, dm), lambda i: (0, i, 0)),
                      pl.BlockSpec((tm, LANES), lambda i: (i, 0))]
        args += [mods, ln[0].reshape(1, dm), ln[1].reshape(1, dm)]
        specs += [pl.BlockSpec((1, 1, N_MOD * dm), mod_row(gate[0])), vec, vec]
    if mod is not None:
        args.append(mods)
        specs.append(pl.BlockSpec((1, 1, N_MOD * dm), mod_row(mod[0])))
    if router is not None:
        w_router, b_router = router
        args += [w_router, b_router.reshape(1, N_EXPERTS)]
        specs += [pl.BlockSpec((dm, N_EXPERTS), lambda i: (0, 0)), pl.BlockSpec((1, N_EXPERTS), lambda i: (0, 0))]
    dtypes = {'h': (dm, F32), 'u16': (dm, BF16), 'idx': (LANES, jnp.int32), 'gates': (LANES, F32)}
    res = pl.pallas_call(
        functools.partial(_norm_mod_kernel, dm=dm, gate=None if gate is None else gate[1],
                          mod=None if mod is None else mod[1:], has_norm=gate is not None,
                          combine=slot_gates is not None, route=router is not None, outs=outs),
        grid=(m // tm,),
        in_specs=specs,
        out_specs=[pl.BlockSpec((tm, dtypes[k][0]), lambda i: (i, 0)) for k in outs],
        out_shape=[jax.ShapeDtypeStruct((m,) + dtypes[k][:1], dtypes[k][1]) for k in outs],
        compiler_params=_cparams(("parallel",)),
    )(*args)
    return res[0] if len(outs) == 1 else res


SUBLANES = 8


def _shift_kernel(z_ref, prev_ref, next_ref, mu_ref, o_ref, *, per_seq, ctx_tiles):
    j = pl.program_id(0) % per_seq
    z = z_ref[...]
    tm = z.shape[0]
    starts_part = jnp.logical_or(j == 0, j == ctx_tiles)
    ends_part = jnp.logical_or(j == ctx_tiles - 1, j == per_seq - 1)
    before = prev_ref[SUBLANES - 1:SUBLANES, :] * jnp.where(starts_part, 0.0, 1.0)
    after = next_ref[0:1, :] * jnp.where(ends_part, 0.0, 1.0)
    row = lax.broadcasted_iota(jnp.int32, (tm, 1), 0)
    prev = jnp.where(row == 0, before, pltpu.roll(z, 1, axis=0))
    nxt = jnp.where(row == tm - 1, after, pltpu.roll(z, tm - 1, axis=0))
    o_ref[...] = z + mu_ref[...] * (0.5 * (prev + nxt) - z)


def token_shift(z, mu, rows_per_seq, n_ctx, tm=256):
    m, width = z.shape
    tm = _row_tile(n_ctx, tm)
    per_tile = tm // SUBLANES
    last = m // SUBLANES - 1
    return pl.pallas_call(
        functools.partial(_shift_kernel, per_seq=rows_per_seq // tm, ctx_tiles=n_ctx // tm),
        grid=(m // tm,),
        in_specs=[pl.BlockSpec((tm, width), lambda i: (i, 0)),
                  pl.BlockSpec((SUBLANES, width), lambda i: (jnp.maximum(i * per_tile - 1, 0), 0)),
                  pl.BlockSpec((SUBLANES, width), lambda i: (jnp.minimum((i + 1) * per_tile, last), 0)),
                  pl.BlockSpec((1, width), lambda i: (0, 0))],
        out_specs=pl.BlockSpec((tm, width), lambda i: (i, 0)),
        out_shape=jax.ShapeDtypeStruct((m, width), F32),
        compiler_params=_cparams(("parallel",)),
    )(z, z, z, mu.reshape(1, width))


def _even_mixer(u, b, t, n_ctx, w_in, mu, w0, w2, a0, a2, g2, k_k, k_a, r_k, gn_g, gn_b, lower,
                hg_norm_g, w_out):
    za, zb = matmul(u, w_in, splits=(RW_IN, HG_IN))
    za = token_shift(za, mu, t, n_ctx).reshape(b, t, RW_IN)
    zb = zb.reshape(b, t, HG_IN)
    yf, yb = rwkv7_scan(za, w0, w2, a0, a2, k_k, k_a, n_ctx)
    of, ob = hgrn2_scan(zb, lower, n_ctx)
    flat = lambda x: x.reshape(b * t, x.shape[-1])
    y = even_post(flat(za), flat(yf), flat(yb), flat(zb), flat(of), flat(ob), a0, a2, g2, k_a,
                  r_k.reshape(-1), gn_g, gn_b, hg_norm_g, w_out)
    return y


def _rope_tables(rows):
    t = jnp.arange(rows * GRID_W)
    row = (t // GRID_W).astype(F32)
    col = (t % GRID_W).astype(F32)
    half = MLA_ROPE // 2
    inv_freq = ROPE_BASE ** (-jnp.arange(0, half, 2, dtype=F32) / half)
    ang_r = row[:, None] * inv_freq
    ang_c = col[:, None] * inv_freq
    ang = jnp.concatenate([ang_r, ang_r, ang_c, ang_c], axis=-1)
    return jnp.cos(ang), jnp.sin(ang)


def _mla_weights(w_in, w_qb, w_kvb, w_out):
    hh, half = MLA_HEADS, MLA_HEADS // 2
    w_in_ext = jnp.concatenate([w_in, _rot_half(w_in[:, MLA_Q_RANK + MLA_KV_RANK:])], axis=1)
    kvb = w_kvb.reshape(MLA_KV_RANK, half, 2, 2, MLA_NOPE)
    kv128 = jnp.stack([kvb[:, :, 0], kvb[:, :, 1, ::-1]], axis=2).reshape(MLA_KV_RANK, hh, LANES)
    kv_rows = jnp.pad(kv128, ((0, 0), (0, 0), (0, MLA_SLOT - LANES))).reshape(MLA_KV_RANK, hh * MLA_SLOT)
    rope_rows = jnp.pad(jnp.eye(MLA_ROPE, dtype=F32), ((0, 0), (LANES, MLA_SLOT - LANES - MLA_ROPE)))
    w_kv_slots = jnp.concatenate([kv_rows, jnp.tile(rope_rows, (1, hh))], axis=0).astype(BF16)
    qb = w_qb.reshape(MLA_Q_RANK, hh, MLA_QK)
    nope = qb[..., :MLA_NOPE].reshape(MLA_Q_RANK, half, 2, MLA_NOPE)
    zeros = jnp.zeros_like(nope[:, :, 0])
    nope128 = jnp.stack([jnp.concatenate([nope[:, :, 0], zeros], -1),
                         jnp.concatenate([zeros, nope[:, :, 1]], -1)], axis=2).reshape(MLA_Q_RANK, hh, LANES)
    rope = qb[..., MLA_NOPE:]
    pad_rope = lambda x: jnp.pad(x, ((0, 0), (0, 0), (0, LANES - MLA_ROPE)))
    wq_slots = jnp.concatenate([nope128, pad_rope(rope)], -1).reshape(MLA_Q_RANK, hh * MLA_SLOT)
    wr_slots = pad_rope(_rot_half(rope)).reshape(MLA_Q_RANK, hh * LANES)
    w_out_perm = w_out.reshape(half, 2, MLA_V, -1)[:, ::-1].reshape(hh * MLA_V, -1)
    return w_in_ext, w_kv_slots, wq_slots.astype(BF16), wr_slots.astype(BF16), w_out_perm


def _mla_mixer(u, b, t, n_ctx, w_in, q_norm_g, w_qb, kv_norm_g, w_kvb, w_out, cos, sin):
    w_in_ext, w_kv_slots, wq_slots, wr_slots, w_out_perm = _mla_weights(w_in, w_qb, w_kvb, w_out)
    zq, z_kv = matmul(u, w_in_ext, splits=(MLA_Q_RANK, MLA_KV_RANK + 2 * MLA_ROPE))
    ones, zeros = jnp.ones((n_ctx, MLA_ROPE), F32), jnp.zeros((n_ctx, MLA_ROPE), F32)
    cs = jnp.concatenate([jnp.concatenate([ones, cos], axis=0), jnp.concatenate([zeros, sin], axis=0)], axis=1)
    kv_slots = mla_kv(z_kv, cs, kv_norm_g, w_kv_slots, t).reshape(b, t, MLA_HEADS * MLA_SLOT)
    zq = zq.reshape(b, t, MLA_Q_RANK)
    widen = lambda x, fill: jnp.pad(x, ((0, 0), (0, LANES - MLA_ROPE)), constant_values=fill)
    o_ctx = mla_attention(zq[:, :n_ctx], kv_slots, n_ctx, widen(ones, 1.0), widen(zeros, 0.0), q_norm_g,
                          wq_slots, wr_slots)
    o_lat = mla_attention(zq[:, n_ctx:], kv_slots, t, widen(cos, 1.0), widen(sin, 0.0), q_norm_g,
                          wq_slots, wr_slots)
    o = jnp.concatenate([o_ctx, o_lat], axis=1).reshape(b * t, MLA_HEADS * MLA_V)
    return matmul(o, w_out_perm)


def kernel(x, c, ctx, c_ctx, mod_w, mod_b, ln_g, ln_b, ev_w_in, rw_mu, rw_w0, rw_w2, rw_a0, rw_a2,
           rw_g2, rw_k_k, rw_k_a, rw_r_k, rw_gn_g, rw_gn_b, hg_lb, hg_norm_g, ev_w_out, od_w_in,
           mla_q_norm_g, mla_w_qb, mla_kv_norm_g, mla_w_kvb, od_w_out, moe_w_router, moe_b_router,
           moe_w_gu, moe_b_gu, moe_w_down, moe_b_down):
    b, n_lat, dm = x.shape
    n_ctx = ctx.shape[1]
    t = n_ctx + n_lat
    cos, sin = _rope_tables(n_lat // GRID_W)
    lb = jax.nn.softmax(hg_lb.astype(F32), axis=0)
    hg_lower = jnp.cumsum(lb, axis=0) - lb[0]
    c_act = c * jax.nn.sigmoid(c)
    cc_act = c_ctx * jax.nn.sigmoid(c_ctx)
    mod_in = jnp.concatenate([c_act, cc_act[None]], axis=0)
    pad = (-mod_in.shape[0]) % 8
    mod_in = jnp.pad(mod_in, ((0, pad), (0, 0)))
    mods = jnp.stack([matmul(mod_in, mod_w[layer]) + mod_b[layer] for layer in range(DEPTH)])
    mods = mods.reshape(DEPTH * mod_in.shape[0], 1, N_MOD * dm)
    nm = functools.partial(norm_mod, mods=mods, rows_per_seq=t, n_ctx=n_ctx, n_batch=b)
    h = jnp.concatenate([ctx, x], axis=1).reshape(b * t, dm)
    u16 = nm(h, None, mod=(0, 0, 1), outs=('u16',))
    for layer in range(DEPTH):
        j = layer // 2
        if layer % 2 == 0:
            y = _even_mixer(u16, b, t, n_ctx, ev_w_in[j], rw_mu[j], rw_w0[j], rw_w2[j], rw_a0[j],
                            rw_a2[j], rw_g2[j], rw_k_k[j], rw_k_a[j], rw_r_k[j], rw_gn_g[j], rw_gn_b[j],
                            hg_lower[j], hg_norm_g[j], ev_w_out[j])
        else:
            y = _mla_mixer(u16, b, t, n_ctx, od_w_in[j], mla_q_norm_g[j], mla_w_qb[j],
                           mla_kv_norm_g[j], mla_w_kvb[j], od_w_out[j], cos, sin)
        h, u16, idx, gts = nm(h, y, gate=(layer, 2), ln=(ln_g[layer, 0], ln_b[layer, 0]),
                              mod=(layer, 3, 4), outs=('h', 'u16', 'idx', 'gates'),
                              router=(moe_w_router[layer], moe_b_router[layer]))
        f, fg = moe(idx, gts, u16, layer, moe_w_gu, moe_b_gu, moe_w_down, moe_b_down)
        if layer + 1 < DEPTH:
            h, u16 = nm(h, f, gate=(layer, 5), ln=(ln_g[layer, 1], ln_b[layer, 1]),
                        mod=(layer + 1, 0, 1), outs=('h', 'u16'), slot_gates=fg)
        else:
            h = nm(h, f, gate=(layer, 5), ln=(ln_g[layer, 1], ln_b[layer, 1]), outs=('h',),
                   slot_gates=fg)
    return h.reshape(b, t, dm)[:, n_ctx:]
```

```python
import functools

import jax
import jax.numpy as jnp
from jax import lax
from jax.experimental import pallas as pl
from jax.experimental.pallas import tpu as pltpu

F32 = jnp.float32
BF16 = jnp.bfloat16
HIGHEST = lax.Precision.HIGHEST

DEPTH = 4
GRID_W = 64
N_MOD = 6

RW_HEADS = 8
RW_HEAD_DIM = 64
RW_DIM = RW_HEADS * RW_HEAD_DIM
RW_DECAY_LORA = 64
RW_ICLR_LORA = 64
RW_GATE_LORA = 128
RW_GN_EPS = 64e-5
RW_IN = 3 * RW_DIM + 2 * RW_DECAY_LORA + 2 * RW_ICLR_LORA + RW_GATE_LORA
RW_LORA_OFF = 3 * RW_DIM

HG_HEADS = 4
HG_KEY_DIM = 128
HG_DIM = HG_HEADS * HG_KEY_DIM
HG_VDIM = HG_DIM
HG_IN = 3 * HG_DIM + 2 * HG_VDIM
HG_F_FLOOR = 1e-30

MLA_HEADS = 16
MLA_NOPE = 64
MLA_ROPE = 32
MLA_V = 64
MLA_Q_RANK = 256
MLA_KV_RANK = 128
MLA_QK = MLA_NOPE + MLA_ROPE
MLA_SCALE = MLA_QK ** -0.5
ROPE_BASE = 10000.0

N_EXPERTS = 32
TOP_K = 4
EXPERT_DIM = 1024
SWIGLU_LIMIT = 7.0
SWIGLU_ALPHA = 1.702

DEEPNORM_ALPHA = (2 * DEPTH) ** 0.25
LN_EPS = 1e-5
RMS_EPS = 1e-6

LANES = 128
SCAN_CHUNK = 64
HG_SUB = 16
MOE_TILE = 512
RANK_BLOCK = 256
MOE_PARTS = 2
VMEM_LIMIT = 56 * 1024 * 1024


def _cparams(sem):
    return pltpu.CompilerParams(dimension_semantics=sem, vmem_limit_bytes=VMEM_LIMIT)


def _row_tile(m, tm):
    if m <= tm:
        return m
    while m % tm:
        tm //= 2
    assert tm >= 8
    return tm


def _mm_kernel(x_ref, w_ref, *o_refs):
    acc = jnp.dot(x_ref[...].astype(BF16), w_ref[...], preferred_element_type=F32)
    off = 0
    for o_ref in o_refs:
        width = o_ref.shape[1]
        o_ref[...] = acc[:, off:off + width].astype(o_ref.dtype)
        off += width


def matmul(x, w, splits=None, out_dtype=F32, tm=512):
    m, k = x.shape
    n = w.shape[1]
    tm = _row_tile(m, tm)
    widths = (n,) if splits is None else splits
    assert sum(widths) == n
    outs = pl.pallas_call(
        _mm_kernel,
        grid=(m // tm,),
        in_specs=[pl.BlockSpec((tm, k), lambda i: (i, 0)),
                  pl.BlockSpec((k, n), lambda i: (0, 0))],
        out_specs=[pl.BlockSpec((tm, wd), lambda i: (i, 0)) for wd in widths],
        out_shape=[jax.ShapeDtypeStruct((m, wd), out_dtype) for wd in widths],
        compiler_params=_cparams(("parallel",)),
    )(x, w.astype(BF16))
    return outs[0] if splits is None else outs


def _route_top4(u, w, b):
    x = jnp.dot(u, w, preferred_element_type=F32, precision=HIGHEST) + b
    tm, ne = x.shape
    lane = lax.broadcasted_iota(jnp.int32, (tm, ne), 1).astype(F32)
    slot = lax.broadcasted_iota(jnp.int32, (tm, LANES), 1)
    idx = jnp.zeros((tm, LANES), F32)
    tops = []
    for k in range(TOP_K):
        mx = jnp.max(x, axis=-1, keepdims=True)
        ix = jnp.min(jnp.where(x == mx, lane, float(ne)), axis=-1, keepdims=True)
        tops.append(mx)
        idx = jnp.where(slot == k, ix, idx)
        x = jnp.where(lane == ix, -jnp.inf, x)
    e = [jnp.exp(v - tops[0]) for v in tops]
    inv = 1.0 / sum(e)
    gates = jnp.zeros((tm, LANES), F32)
    for k in range(TOP_K):
        gates = jnp.where(slot == k, e[k] * inv, gates)
    return idx.astype(jnp.int32), gates


def _bwd_chunk(c, n_ctx_chunks, n_chunks):
    return jnp.where(c < n_ctx_chunks, n_ctx_chunks - 1 - c, n_chunks - 1 - (c - n_ctx_chunks))


def _dot(a, b):
    return jnp.dot(a.astype(BF16), b.astype(BF16), preferred_element_type=F32)


def _dot_nt(a, b):
    return lax.dot_general(a.astype(BF16), b.astype(BF16), (((1,), (1,)), ((), ())),
                           preferred_element_type=F32)


def _dot_tn(a, b):
    return lax.dot_general(a.astype(BF16), b.astype(BF16), (((0,), (0,)), ((), ())),
                           preferred_element_type=F32)


def _dot_f32(a, b):
    return jnp.dot(a, b, preferred_element_type=F32, precision=HIGHEST)


def _seg_sum(x, ones_bd):
    hi = x.astype(BF16)
    lo = (x - hi.astype(F32)).astype(BF16)
    return (jnp.dot(hi, ones_bd, preferred_element_type=F32)
            + jnp.dot(lo, ones_bd, preferred_element_type=F32))


def _block_diag_ones(width, block):
    i = jnp.arange(width) // block
    return (i[:, None] == i[None, :]).astype(BF16)


def _hgrn_kernel(zf_ref, zb_ref, lo_ref, bd_ref, of_ref, ob_ref, st_ref):
    c = pl.program_id(1)
    n, sub = SCAN_CHUNK, HG_SUB
    nb = n // sub

    @pl.when(c == 0)
    def _():
        st_ref[...] = jnp.zeros_like(st_ref)

    row = lax.broadcasted_iota(jnp.int32, (n, n), 0)
    col = lax.broadcasted_iota(jnp.int32, (n, n), 1)
    colb = lax.broadcasted_iota(jnp.int32, (sub, LANES), 1)
    rown = lax.broadcasted_iota(jnp.int32, (sub, n), 0)
    coln = lax.broadcasted_iota(jnp.int32, (sub, n), 1)
    bd2 = bd_ref[...]

    dirs = []
    for d, z_ref in ((0, zf_ref), (1, zb_ref)):
        z = z_ref[0]
        lower = lo_ref[d]
        qraw = z[:, :HG_DIM]
        sig = jax.nn.sigmoid(z[:, HG_DIM * (1 + d):HG_DIM * (2 + d)])
        v = z[:, 3 * HG_DIM:3 * HG_DIM + HG_VDIM]
        f = lower + (1.0 - lower) * sig
        g = jnp.log(jnp.maximum(f, HG_F_FLOOR))
        kin = (1.0 - lower) * (1.0 - sig)
        q = qraw * jax.nn.sigmoid(qraw)
        lag = (row - col) if d == 0 else (col - row)
        cum = _dot_f32((lag >= 0).astype(F32), g)
        cum_ex = cum - g
        tot = jnp.sum(g, axis=0, keepdims=True)
        blocks = []
        for j in range(nb):
            lo, hi = j * sub, (j + 1) * sub
            q_j, cum_j = q[lo:hi], cum[lo:hi]
            ref_j = cum_ex[lo:lo + 1] if d == 0 else cum_ex[hi - 1:hi]
            qt = q_j * jnp.exp(cum_j - ref_j)
            kt = kin * jnp.exp(jnp.minimum(ref_j - cum, 0.0))
            pair = jnp.concatenate(
                [q_j * kin[s:s + 1] * jnp.exp(jnp.minimum(cum_j - cum[s:s + 1], 0.0))
                 for s in range(lo, hi)], axis=0)
            blocks.append((qt, kt, pair))
        dirs.append(dict(d=d, v=v, qe=q * jnp.exp(cum), kdec=kin * jnp.exp(tot - cum),
                         etot=jnp.exp(tot), blocks=blocks))

    for dd, o_ref in zip(dirs, (of_ref, ob_ref)):
        d = dd['d']
        for j, (qt, kt, pair) in enumerate(dd['blocks']):
            lo, hi = j * sub, (j + 1) * sub
            sums = [_dot(pair[:, 2 * LANES * hp:2 * LANES * (hp + 1)], bd2) for hp in range(HG_HEADS // 2)]
            for h in range(HG_HEADS):
                hs = slice(LANES * h, LANES * (h + 1))
                rs = sums[h // 2][:, LANES * (h % 2):LANES * (h % 2 + 1)]
                diag = jnp.zeros((sub, LANES), F32)
                for s in range(sub):
                    diag = jnp.where(colb == lo + s, rs[s * sub:(s + 1) * sub], diag)
                off = _dot_nt(qt[:, hs], kt[:, hs])
                if d == 0:
                    earlier, ordered = coln < lo, rown + lo >= coln
                else:
                    earlier, ordered = coln >= hi, rown + lo <= coln
                sc = jnp.where(earlier, off, jnp.where(ordered, diag[:, :n], 0.0))
                dd.setdefault(('out', h), []).append(_dot(sc, dd['v'][:, hs]))

    for dd, o_ref in zip(dirs, (of_ref, ob_ref)):
        d = dd['d']
        for h in range(HG_HEADS):
            hs = slice(LANES * h, LANES * (h + 1))
            st = st_ref[d * HG_HEADS + h]
            out = jnp.concatenate(dd[('out', h)], axis=0) + _dot_nt(dd['qe'][:, hs], st)
            o_ref[0, :, hs] = out
            st_ref[d * HG_HEADS + h] = (st * dd['etot'][:, hs]
                                        + _dot_tn(dd['v'][:, hs], dd['kdec'][:, hs]))


def hgrn2_scan(zb, lower, n_ctx):
    b, t, _ = zb.shape
    n = SCAN_CHUNK
    nc, ncc = t // n, n_ctx // n
    bwd = functools.partial(_bwd_chunk, n_ctx_chunks=ncc, n_chunks=nc)
    const = lambda shape: pl.BlockSpec(shape, lambda bi, c: (0,) * len(shape))
    return pl.pallas_call(
        _hgrn_kernel,
        grid=(b, nc),
        in_specs=[pl.BlockSpec((1, n, HG_IN), lambda bi, c: (bi, c, 0)),
                  pl.BlockSpec((1, n, HG_IN), lambda bi, c: (bi, bwd(c), 0)),
                  const((2, 1, HG_DIM)), const((2 * LANES, 2 * LANES))],
        out_specs=[pl.BlockSpec((1, n, HG_VDIM), lambda bi, c: (bi, c, 0)),
                   pl.BlockSpec((1, n, HG_VDIM), lambda bi, c: (bi, bwd(c), 0))],
        out_shape=[jax.ShapeDtypeStruct((b, t, HG_VDIM), F32)] * 2,
        scratch_shapes=[pltpu.VMEM((2 * HG_HEADS, HG_KEY_DIM, HG_KEY_DIM), F32)],
        compiler_params=_cparams(("parallel", "arbitrary")),
    )(zb, zb, lower.reshape(2, 1, HG_DIM), _block_diag_ones(2 * LANES, LANES))


def _rwkv_iclr(z, d, a0_ref, a2_ref):
    ad = z[:, RW_LORA_OFF + LANES:RW_LORA_OFF + 2 * LANES]
    return jax.nn.sigmoid(a0_ref[d] + _dot(ad, a2_ref[d]))


def _rwkv_scan_kernel(zf_ref, zb_ref, w0_ref, w2_ref, a0_ref, a2_ref, kk_ref, ka_ref, bd_ref,
                      yf_ref, yb_ref, st_ref):
    c = pl.program_id(1)
    n, cdim = SCAN_CHUNK, RW_DIM
    npair = cdim // LANES

    @pl.when(c == 0)
    def _():
        st_ref[...] = jnp.zeros_like(st_ref)

    row = lax.broadcasted_iota(jnp.int32, (n, n), 0)
    col = lax.broadcasted_iota(jnp.int32, (n, n), 1)
    row4 = lax.broadcasted_iota(jnp.int32, (2 * n, 4 * n), 0) % n
    col4 = lax.broadcasted_iota(jnp.int32, (2 * n, 4 * n), 1) % n
    lane = lax.broadcasted_iota(jnp.int32, (1, LANES), 1)
    half = [lane < RW_HEAD_DIM, lane >= RW_HEAD_DIM]

    def stack2(x):
        return jnp.concatenate([jnp.where(half[0], x, 0.0), jnp.where(half[1], x, 0.0)], axis=0)

    r2 = lax.broadcasted_iota(jnp.int32, (LANES, LANES), 0) < RW_HEAD_DIM
    c2 = lax.broadcasted_iota(jnp.int32, (LANES, LANES), 1) < RW_HEAD_DIM
    same_head = r2 == c2

    chains = []
    for d, z_ref in ((0, zf_ref), (1, zb_ref)):
        z = z_ref[0]
        r, k, v = z[:, :cdim], z[:, cdim:2 * cdim], z[:, 2 * cdim:3 * cdim]
        wd = z[:, RW_LORA_OFF:RW_LORA_OFF + LANES]
        w_pre = w0_ref[d] + _dot(jnp.tanh(wd), w2_ref[d])
        lw = -jnp.exp(-0.5) * jax.nn.sigmoid(w_pre)
        a = _rwkv_iclr(z, d, a0_ref, a2_ref)
        kk = k * kk_ref[...]
        kk = kk * lax.rsqrt(jnp.maximum(_seg_sum(kk * kk, bd_ref[...]), 1e-24))
        kd = k * (1.0 + (a - 1.0) * ka_ref[...])
        bb = kk * a
        lag = (row - col) if d == 0 else (col - row)
        lag4 = (row4 - col4) if d == 0 else (col4 - row4)
        incl4, strict4 = lag4 >= 0, lag4 > 0
        cum = _dot_f32((lag >= 0).astype(F32), lw)
        tot = jnp.sum(lw, axis=0, keepdims=True)
        g_inv = jnp.exp(-cum)
        g_tail = jnp.exp(tot - cum)
        rt = r * jnp.exp(cum)
        at = -kk * jnp.exp(cum - lw)
        etot = jnp.exp(tot)
        bt, kt = bb * g_inv, kd * g_inv
        tail = jnp.concatenate([bb * g_tail, kd * g_tail], axis=0)
        for p in range(npair):
            ps = slice(LANES * p, LANES * (p + 1))
            chains.append(dict(d=d, p=p, ps=ps, incl4=incl4, strict4=strict4, tail=tail[:, ps],
                               at=stack2(at[:, ps]), rt=rt[:, ps], bt=bt[:, ps], kt=kt[:, ps],
                               v=v[:, ps], etot=etot[:, ps]))

    for ch in chains:
        left = jnp.concatenate([ch['at'], stack2(ch['rt'])], axis=0)
        right = jnp.concatenate([stack2(ch['bt']), stack2(ch['kt'])], axis=0)
        prod = _dot_nt(left, right)
        ch['top'] = jnp.where(ch['strict4'], prod[:2 * n], 0.0)
        ch['bot'] = jnp.where(ch['incl4'], prod[2 * n:], 0.0)
    for ch in chains:
        ch['vs'] = stack2(ch['v'])
        ch['x'] = jnp.concatenate([ch['at'], _dot(ch['top'][:, 2 * n:], ch['vs'])], axis=1)
        ch['a'] = ch['top'][:, :2 * n]
    steps = n.bit_length() - 1
    for i in range(steps):
        for ch in chains:
            ch['x'] = ch['x'] + _dot(ch['a'], ch['x'])
            if i + 1 < steps:
                ch['a'] = _dot(ch['a'], ch['a'])
    for ch in chains:
        x = ch['x'][:n] + ch['x'][n:]
        st = st_ref[ch['d'] * npair + ch['p']]
        ws = _dot_nt(jnp.concatenate([x[:, :LANES], ch['rt']], axis=0), st)
        u = ws[:n] + x[:, LANES:]
        ys = _dot(ch['bot'], jnp.concatenate([stack2(u), ch['vs']], axis=0))
        y_ref = yf_ref if ch['d'] == 0 else yb_ref
        y_ref[0, :, ch['ps']] = ws[n:] + ys[:n] + ys[n:]
        delta = _dot_tn(jnp.concatenate([u, ch['v']], axis=0), ch['tail'])
        st_ref[ch['d'] * npair + ch['p']] = st * ch['etot'] + jnp.where(same_head, delta, 0.0)


def _pad_lora(w, d):
    return jnp.concatenate([w[d] if i == d else jnp.zeros_like(w[i]) for i in range(2)], axis=0)


def rwkv7_scan(za, w0, w2, a0, a2, k_k, k_a, n_ctx):
    b, t, _ = za.shape
    n, cdim = SCAN_CHUNK, RW_DIM
    nc, ncc = t // n, n_ctx // n
    bwd = functools.partial(_bwd_chunk, n_ctx_chunks=ncc, n_chunks=nc)
    const = lambda shape: pl.BlockSpec(shape, lambda bi, c: (0,) * len(shape))
    w2p = jnp.stack([_pad_lora(w2, d) for d in range(2)]).astype(BF16)
    a2p = jnp.stack([_pad_lora(a2, d) for d in range(2)]).astype(BF16)
    return pl.pallas_call(
        _rwkv_scan_kernel,
        grid=(b, nc),
        in_specs=[pl.BlockSpec((1, n, RW_IN), lambda bi, c: (bi, c, 0)),
                  pl.BlockSpec((1, n, RW_IN), lambda bi, c: (bi, bwd(c), 0)),
                  const((2, 1, cdim)), const((2, LANES, cdim)), const((2, 1, cdim)),
                  const((2, LANES, cdim)), const((1, cdim)), const((1, cdim)), const((cdim, cdim))],
        out_specs=[pl.BlockSpec((1, n, cdim), lambda bi, c: (bi, c, 0)),
                   pl.BlockSpec((1, n, cdim), lambda bi, c: (bi, bwd(c), 0))],
        out_shape=[jax.ShapeDtypeStruct((b, t, cdim), F32)] * 2,
        scratch_shapes=[pltpu.VMEM((2 * cdim // LANES, LANES, LANES), F32)],
        compiler_params=_cparams(("parallel", "arbitrary")),
    )(za, za, w0.reshape(2, 1, cdim), w2p, a0.reshape(2, 1, cdim), a2p, k_k.reshape(1, cdim),
      k_a.reshape(1, cdim), _block_diag_ones(cdim, RW_HEAD_DIM))


def _even_post_kernel(za_ref, yf_ref, yb_ref, zg_ref, of_ref, ob_ref, a0_ref, a2_ref, g2_ref,
                      ka_ref, rk_ref, gng_ref, gnb_ref, hgg_ref, bd64_ref, bd128_ref, wout_ref,
                      o_ref):
    cdim = RW_DIM
    z = za_ref[...]
    r, k, v = z[:, :cdim], z[:, cdim:2 * cdim], z[:, 2 * cdim:3 * cdim]
    a_sum = _rwkv_iclr(z, 0, a0_ref, a2_ref) + _rwkv_iclr(z, 1, a0_ref, a2_ref)
    k_sum = k * (2.0 + (a_sum - 2.0) * ka_ref[...])
    gate = _dot(jax.nn.sigmoid(z[:, RW_LORA_OFF + 2 * LANES:]), g2_ref[...])
    bd64 = bd64_ref[...]
    y = yf_ref[...] + yb_ref[...]
    mean = _seg_sum(y, bd64) * (1.0 / RW_HEAD_DIM)
    cen = y - mean
    var = _seg_sum(cen * cen, bd64) * (1.0 / RW_HEAD_DIM)
    y = cen * lax.rsqrt(var + RW_GN_EPS) * gng_ref[...] + gnb_ref[...]
    bonus = _seg_sum(r * k_sum * rk_ref[...], bd64) * v
    y_rw = (y + bonus) * gate

    o = of_ref[...] + ob_ref[...]
    ms = _seg_sum(o * o, bd128_ref[...]) * (1.0 / HG_KEY_DIM)
    g_hg = zg_ref[...]
    y_hg = o * lax.rsqrt(ms + RMS_EPS) * hgg_ref[...] * (g_hg * jax.nn.sigmoid(g_hg))

    y_all = jnp.concatenate([y_rw, y_hg], axis=1).astype(BF16)
    o_ref[...] = jnp.dot(y_all, wout_ref[...], preferred_element_type=F32)


def even_post(za, yf, yb, zb, of, ob, a0, a2, g2, k_a, r_k, gn_g, gn_b, hg_norm_g, w_out, tm=256):
    m = za.shape[0]
    cdim, dm = RW_DIM, w_out.shape[1]
    tm = _row_tile(m, tm)
    rows = lambda width, blk=0: pl.BlockSpec((tm, width), lambda i: (i, blk))
    const = lambda shape: pl.BlockSpec(shape, lambda i: (0,) * len(shape))
    a2p = jnp.stack([_pad_lora(a2, d) for d in range(2)]).astype(BF16)
    vec = lambda x: x.reshape(1, cdim)
    return pl.pallas_call(
        _even_post_kernel,
        grid=(m // tm,),
        in_specs=[rows(RW_IN), rows(cdim), rows(cdim), rows(HG_VDIM, (HG_IN - HG_VDIM) // HG_VDIM),
                  rows(HG_VDIM), rows(HG_VDIM),
                  const((2, 1, cdim)), const((2, LANES, cdim)), const((RW_GATE_LORA, cdim)),
                  const((1, cdim)), const((1, cdim)), const((1, cdim)), const((1, cdim)),
                  const((1, HG_VDIM)), const((cdim, cdim)), const((HG_VDIM, HG_VDIM)),
                  const((cdim + HG_VDIM, dm))],
        out_specs=rows(dm),
        out_shape=jax.ShapeDtypeStruct((m, dm), F32),
        compiler_params=_cparams(("parallel",)),
    )(za, yf, yb, zb, of, ob, a0.reshape(2, 1, cdim), a2p, g2.astype(BF16), vec(k_a), vec(r_k),
      vec(gn_g), vec(gn_b), hg_norm_g.reshape(1, HG_VDIM), _block_diag_ones(cdim, RW_HEAD_DIM),
      _block_diag_ones(HG_VDIM, HG_KEY_DIM), w_out.astype(BF16))


MLA_SLOT = 2 * LANES
MLA_GROUP = 4


def _rot_half(w):
    r1, r2, c1, c2 = jnp.split(w, 4, axis=-1)
    return jnp.concatenate([-r2, r1, -c2, c1], axis=-1)


def _mla_kv_kernel(z_ref, cs_ref, g_ref, w_ref, o_ref):
    z = z_ref[...]
    kva = z[:, :MLA_KV_RANK]
    kva = kva * lax.rsqrt(jnp.mean(kva * kva, axis=-1, keepdims=True) + RMS_EPS) * g_ref[...]
    cs = cs_ref[...]
    kpe = (z[:, MLA_KV_RANK:MLA_KV_RANK + MLA_ROPE] * cs[:, :MLA_ROPE]
           + z[:, MLA_KV_RANK + MLA_ROPE:] * cs[:, MLA_ROPE:])
    x = jnp.concatenate([kva, kpe], axis=1).astype(BF16)
    o_ref[...] = jnp.dot(x, w_ref[...], preferred_element_type=F32).astype(o_ref.dtype)


def mla_kv(z_kv, cs, kv_norm_g, w_kv_slots, rows_per_seq, tm=256):
    m, width = z_kv.shape
    tm = _row_tile(rows_per_seq, tm)
    per_seq = rows_per_seq // tm
    n_out = w_kv_slots.shape[1]
    return pl.pallas_call(
        _mla_kv_kernel,
        grid=(m // tm,),
        in_specs=[pl.BlockSpec((tm, width), lambda i: (i, 0)),
                  pl.BlockSpec((tm, 2 * MLA_ROPE), lambda i: (i % per_seq, 0)),
                  pl.BlockSpec((1, MLA_KV_RANK), lambda i: (0, 0)),
                  pl.BlockSpec(w_kv_slots.shape, lambda i: (0, 0))],
        out_specs=pl.BlockSpec((tm, n_out), lambda i: (i, 0)),
        out_shape=jax.ShapeDtypeStruct((m, n_out), BF16),
        compiler_params=_cparams(("parallel",)),
    )(z_kv, cs, kv_norm_g.reshape(1, MLA_KV_RANK), w_kv_slots)


def _mla_attn_kernel(zq_ref, kv_ref, cos_ref, sin_ref, g_ref, wq_ref, wr_ref, o_ref):
    zq = zq_ref[0]
    zn = (zq * lax.rsqrt(jnp.mean(zq * zq, axis=-1, keepdims=True) + RMS_EPS) * g_ref[...]).astype(BF16)
    a = jnp.dot(zn, wq_ref[...], preferred_element_type=F32)
    ar = jnp.dot(zn, wr_ref[...], preferred_element_type=F32)
    cos, sin = cos_ref[...], sin_ref[...]
    low = lax.broadcasted_iota(jnp.int32, (1, LANES), 1) < MLA_V
    blocks = []
    for pair in range(MLA_GROUP // 2):
        res = []
        for j in range(2):
            h = 2 * pair + j
            q_nope = a[:, MLA_SLOT * h:MLA_SLOT * h + LANES]
            q_rope = (a[:, MLA_SLOT * h + LANES:MLA_SLOT * (h + 1)] * cos
                      + ar[:, LANES * h:LANES * (h + 1)] * sin)
            qf = (jnp.concatenate([q_nope, q_rope], axis=1) * MLA_SCALE).astype(BF16)
            s = lax.dot_general(qf, kv_ref[0, :, MLA_SLOT * h:MLA_SLOT * (h + 1)],
                                (((1,), (1,)), ((), ())), preferred_element_type=F32)
            p = jnp.exp(s - jnp.max(s, axis=-1, keepdims=True))
            l = jnp.sum(p, axis=-1, keepdims=True)
            res.append(jnp.dot(p.astype(BF16), kv_ref[0, :, MLA_SLOT * h:MLA_SLOT * h + LANES],
                               preferred_element_type=F32) / l)
        blocks.append(jnp.where(low, res[1], res[0]))
    o_ref[0] = jnp.concatenate(blocks, axis=1).astype(o_ref.dtype)


def mla_attention(zq, kv_slots, n_keys, cos, sin, q_norm_g, wq_slots, wr_slots, tq=1024):
    b, nq, _ = zq.shape
    tq = _row_tile(nq, tq)
    gw = MLA_GROUP * MLA_SLOT
    return pl.pallas_call(
        _mla_attn_kernel,
        grid=(b, MLA_HEADS // MLA_GROUP, nq // tq),
        in_specs=[pl.BlockSpec((1, tq, MLA_Q_RANK), lambda bi, g, qi: (bi, qi, 0)),
                  pl.BlockSpec((1, n_keys, gw), lambda bi, g, qi: (bi, 0, g)),
                  pl.BlockSpec((tq, LANES), lambda bi, g, qi: (qi, 0)),
                  pl.BlockSpec((tq, LANES), lambda bi, g, qi: (qi, 0)),
                  pl.BlockSpec((1, MLA_Q_RANK), lambda bi, g, qi: (0, 0)),
                  pl.BlockSpec((MLA_Q_RANK, gw), lambda bi, g, qi: (0, g)),
                  pl.BlockSpec((MLA_Q_RANK, gw // 2), lambda bi, g, qi: (0, g))],
        out_specs=pl.BlockSpec((1, tq, MLA_GROUP * MLA_V), lambda bi, g, qi: (bi, qi, g)),
        out_shape=jax.ShapeDtypeStruct((b, nq, MLA_HEADS * MLA_V), BF16),
        compiler_params=_cparams(("parallel", "parallel", "parallel")),
    )(zq, kv_slots, cos, sin, q_norm_g.reshape(1, MLA_Q_RANK), wq_slots, wr_slots)


def _moe_kernel(te_ref, nt_ref, x_ref, wgu_ref, bgu_ref, wd_ref, bd_ref, o_ref, wgu_bf, wd_bf):
    i = pl.program_id(0)
    live = i < nt_ref[0]
    new_expert = jnp.logical_or(i == 0, te_ref[i] != te_ref[jnp.maximum(i - 1, 0)])

    @pl.when(jnp.logical_and(live, new_expert))
    def _():
        wgu_bf[...] = wgu_ref[0, 0].astype(BF16)
        wd_bf[...] = wd_ref[0, 0].astype(BF16)

    @pl.when(live)
    def _():
        h = jnp.dot(x_ref[...], wgu_bf[...], preferred_element_type=F32) + bgu_ref[0, 0]
        glu = jnp.minimum(h[:, :EXPERT_DIM], SWIGLU_LIMIT)
        lin = jnp.clip(h[:, EXPERT_DIM:], -SWIGLU_LIMIT, SWIGLU_LIMIT)
        act = glu * jax.nn.sigmoid(SWIGLU_ALPHA * glu) * (lin + 1.0)
        y = jnp.dot(act.astype(BF16), wd_bf[...], preferred_element_type=F32) + bd_ref[0, 0]
        o_ref[...] = y.astype(o_ref.dtype)

    @pl.when(i >= nt_ref[0])
    def _():
        o_ref[...] = jnp.zeros_like(o_ref)


def moe_grouped(x_sorted, tile_expert, n_tiles_used, layer, w_gu, b_gu, w_down, b_down):
    p, dm = x_sorted.shape
    tm = MOE_TILE
    nl, e, _, f2 = w_gu.shape
    grid_spec = pltpu.PrefetchScalarGridSpec(
        num_scalar_prefetch=2,
        grid=(p // tm,),
        in_specs=[
            pl.BlockSpec((tm, dm), lambda i, te, nt: (i, 0)),
            pl.BlockSpec((1, 1, dm, f2), lambda i, te, nt: (layer, te[i], 0, 0)),
            pl.BlockSpec((1, 1, 1, f2), lambda i, te, nt: (layer, te[i], 0, 0)),
            pl.BlockSpec((1, 1, f2 // 2, dm), lambda i, te, nt: (layer, te[i], 0, 0)),
            pl.BlockSpec((1, 1, 1, dm), lambda i, te, nt: (layer, te[i], 0, 0)),
        ],
        out_specs=pl.BlockSpec((tm, dm), lambda i, te, nt: (i, 0)),
        scratch_shapes=[pltpu.VMEM((dm, f2), BF16), pltpu.VMEM((f2 // 2, dm), BF16)],
    )
    return pl.pallas_call(
        _moe_kernel,
        grid_spec=grid_spec,
        out_shape=jax.ShapeDtypeStruct((p, dm), F32),
        compiler_params=_cparams(("arbitrary",)),
    )(tile_expert, n_tiles_used, x_sorted, w_gu, b_gu.reshape(nl, e, 1, f2), w_down,
      b_down.reshape(nl, e, 1, dm))


def moe(idx_lanes, u16, layer, w_gu, b_gu, w_down, b_down, row0, n):
    dm = u16.shape[1]
    tm = MOE_TILE
    top_idx = idx_lanes[row0:row0 + n, :TOP_K]
    e_flat = top_idx.T.reshape(-1).astype(jnp.int32)
    npair = n * TOP_K
    rb = _row_tile(npair, RANK_BLOCK)
    onehot = (e_flat[:, None] == jnp.arange(N_EXPERTS, dtype=jnp.int32)[None, :])
    oh3 = onehot.astype(BF16).reshape(npair // rb, rb, N_EXPERTS)
    earlier = (jnp.arange(rb)[:, None] > jnp.arange(rb)[None, :]).astype(BF16)
    within = jnp.einsum('ij,bje->bie', earlier, oh3, preferred_element_type=F32)
    blk_tot = jnp.sum(oh3.astype(F32), axis=1)
    blk_off = jnp.cumsum(blk_tot, axis=0) - blk_tot
    counts = jnp.sum(blk_tot, axis=0).astype(jnp.int32)
    padded = ((counts + tm - 1) // tm) * tm
    ends_p = jnp.cumsum(padded)
    starts_p = ends_p - padded
    starts = jnp.cumsum(counts) - counts
    rank = within + blk_off[:, None, :] + starts_p.astype(F32)[None, None, :]
    pos = jnp.sum(oh3.astype(F32) * rank, axis=-1).astype(jnp.int32).reshape(npair)
    p_rows = npair + N_EXPERTS * tm
    n_tiles = p_rows // tm
    tile_start = jnp.arange(n_tiles, dtype=jnp.int32) * tm
    tile_expert = jnp.minimum(jnp.sum((ends_p[None, :] <= tile_start[:, None]).astype(jnp.int32), axis=1),
                              N_EXPERTS - 1)
    n_used = (ends_p[-1] // tm).astype(jnp.int32).reshape(1)
    _, sorted_tok = lax.sort_key_val(e_flat, jnp.arange(npair, dtype=jnp.int32) % n)
    onehot_t = (tile_expert[:, None] == jnp.arange(N_EXPERTS, dtype=jnp.int32)[None, :]).astype(jnp.int32)
    shift = jnp.sum(onehot_t * (starts - starts_p)[None, :], axis=1)
    last = jnp.sum(onehot_t * (starts + counts)[None, :], axis=1)
    rows = jnp.arange(p_rows, dtype=jnp.int32).reshape(n_tiles, tm)
    src = rows + shift[:, None]
    valid = src < last[:, None]
    src_tok = jnp.where(valid, sorted_tok[jnp.where(valid, src, rows % npair).reshape(-1)].reshape(n_tiles, tm),
                        rows % n).reshape(-1)
    x_sorted = u16[src_tok + row0]
    y_sorted = moe_grouped(x_sorted, tile_expert, n_used, layer, w_gu, b_gu, w_down, b_down)
    return y_sorted[pos].reshape(TOP_K, n, dm)


def _norm_mod_kernel(*refs, dm, gate, mod, has_norm, combine, tiles_per_part, route, outs):
    refs = list(refs)
    h = refs.pop(0)[...]
    if has_norm:
        if combine:
            parts = [refs.pop(0) for _ in range(combine)]
            slot_gate = refs.pop(0)[...]
            part = pl.program_id(0) // tiles_per_part
            y = None
            for p, y_ref in enumerate(parts):
                yp = sum(y_ref[k] * slot_gate[:, k:k + 1] for k in range(TOP_K))
                y = yp if y is None else jnp.where(part == p, yp, y)
        else:
            y = refs.pop(0)[...]
        mg = refs.pop(0)
        lng, lnb = refs.pop(0)[...], refs.pop(0)[...]
        x = DEEPNORM_ALPHA * h + mg[0, :, gate * dm:(gate + 1) * dm] * y
        mu = jnp.mean(x, axis=-1, keepdims=True)
        cen = x - mu
        var = jnp.mean(cen * cen, axis=-1, keepdims=True)
        h = cen * lax.rsqrt(var + LN_EPS) * lng + lnb
    if mod is not None:
        mm = refs.pop(0)
        shift, scale = mod
        u = h * (1.0 + mm[0, :, scale * dm:(scale + 1) * dm]) + mm[0, :, shift * dm:(shift + 1) * dm]
    vals = {'h': h}
    if mod is not None:
        vals['u16'] = u
    if route:
        wr, br = refs.pop(0)[...], refs.pop(0)[...]
        vals['idx'], vals['gates'] = _route_top4(u, wr, br)
    for kind, o_ref in zip(outs, refs):
        o_ref[...] = vals[kind].astype(o_ref.dtype)


def norm_mod(h, y, mods, rows_per_seq, n_ctx, n_batch, gate=None, ln=None, mod=None, outs=('h',),
             slot_gates=None, router=None, tm=256):
    m, dm = h.shape
    tm = _row_tile(n_ctx, tm)
    per_seq, ctx_tiles = rows_per_seq // tm, n_ctx // tm
    rows = mods.shape[0] // DEPTH

    def mod_row(layer):
        return lambda i: (layer * rows + jnp.where(i % per_seq < ctx_tiles, n_batch, i // per_seq), 0, 0)

    tile = pl.BlockSpec((tm, dm), lambda i: (i, 0))
    vec = pl.BlockSpec((1, dm), lambda i: (0, 0))
    args, specs = [h], [tile]
    tiles_per_part = 0
    if gate is not None:
        if slot_gates is None:
            args.append(y)
            specs.append(tile)
        else:
            tiles_per_part = m // tm // len(y)
            for p, part in enumerate(y):
                args.append(part)
                specs.append(pl.BlockSpec(
                    (TOP_K, tm, dm),
                    lambda i, p=p: (0, jnp.clip(i - p * tiles_per_part, 0, tiles_per_part - 1), 0)))
            args.append(slot_gates)
            specs.append(pl.BlockSpec((tm, LANES), lambda i: (i, 0)))
        args += [mods, ln[0].reshape(1, dm), ln[1].reshape(1, dm)]
        specs += [pl.BlockSpec((1, 1, N_MOD * dm), mod_row(gate[0])), vec, vec]
    if mod is not None:
        args.append(mods)
        specs.append(pl.BlockSpec((1, 1, N_MOD * dm), mod_row(mod[0])))
    if router is not None:
        w_router, b_router = router
        args += [w_router, b_router.reshape(1, N_EXPERTS)]
        specs += [pl.BlockSpec((dm, N_EXPERTS), lambda i: (0, 0)), pl.BlockSpec((1, N_EXPERTS), lambda i: (0, 0))]
    dtypes = {'h': (dm, F32), 'u16': (dm, BF16), 'idx': (LANES, jnp.int32), 'gates': (LANES, F32)}
    res = pl.pallas_call(
        functools.partial(_norm_mod_kernel, dm=dm, gate=None if gate is None else gate[1],
                          mod=None if mod is None else mod[1:], has_norm=gate is not None,
                          combine=0 if slot_gates is None else len(y), tiles_per_part=tiles_per_part,
                          route=router is not None, outs=outs),
        grid=(m // tm,),
        in_specs=specs,
        out_specs=[pl.BlockSpec((tm, dtypes[k][0]), lambda i: (i, 0)) for k in outs],
        out_shape=[jax.ShapeDtypeStruct((m,) + dtypes[k][:1], dtypes[k][1]) for k in outs],
        compiler_params=_cparams(("parallel",)),
    )(*args)
    return res[0] if len(outs) == 1 else res


SUBLANES = 8


def _shift_kernel(z_ref, prev_ref, next_ref, mu_ref, o_ref, *, per_seq, ctx_tiles):
    j = pl.program_id(0) % per_seq
    z = z_ref[...]
    tm = z.shape[0]
    starts_part = jnp.logical_or(j == 0, j == ctx_tiles)
    ends_part = jnp.logical_or(j == ctx_tiles - 1, j == per_seq - 1)
    before = prev_ref[SUBLANES - 1:SUBLANES, :] * jnp.where(starts_part, 0.0, 1.0)
    after = next_ref[0:1, :] * jnp.where(ends_part, 0.0, 1.0)
    row = lax.broadcasted_iota(jnp.int32, (tm, 1), 0)
    prev = jnp.where(row == 0, before, pltpu.roll(z, 1, axis=0))
    nxt = jnp.where(row == tm - 1, after, pltpu.roll(z, tm - 1, axis=0))
    o_ref[...] = z + mu_ref[...] * (0.5 * (prev + nxt) - z)


def token_shift(z, mu, rows_per_seq, n_ctx, tm=256):
    m, width = z.shape
    tm = _row_tile(n_ctx, tm)
    per_tile = tm // SUBLANES
    last = m // SUBLANES - 1
    return pl.pallas_call(
        functools.partial(_shift_kernel, per_seq=rows_per_seq // tm, ctx_tiles=n_ctx // tm),
        grid=(m // tm,),
        in_specs=[pl.BlockSpec((tm, width), lambda i: (i, 0)),
                  pl.BlockSpec((SUBLANES, width), lambda i: (jnp.maximum(i * per_tile - 1, 0), 0)),
                  pl.BlockSpec((SUBLANES, width), lambda i: (jnp.minimum((i + 1) * per_tile, last), 0)),
                  pl.BlockSpec((1, width), lambda i: (0, 0))],
        out_specs=pl.BlockSpec((tm, width), lambda i: (i, 0)),
        out_shape=jax.ShapeDtypeStruct((m, width), F32),
        compiler_params=_cparams(("parallel",)),
    )(z, z, z, mu.reshape(1, width))


def _even_mixer(u, b, t, n_ctx, w_in, mu, w0, w2, a0, a2, g2, k_k, k_a, r_k, gn_g, gn_b, lower,
                hg_norm_g, w_out):
    za, zb = matmul(u, w_in, splits=(RW_IN, HG_IN))
    za = token_shift(za, mu, t, n_ctx).reshape(b, t, RW_IN)
    zb = zb.reshape(b, t, HG_IN)
    yf, yb = rwkv7_scan(za, w0, w2, a0, a2, k_k, k_a, n_ctx)
    of, ob = hgrn2_scan(zb, lower, n_ctx)
    flat = lambda x: x.reshape(b * t, x.shape[-1])
    y = even_post(flat(za), flat(yf), flat(yb), flat(zb), flat(of), flat(ob), a0, a2, g2, k_a,
                  r_k.reshape(-1), gn_g, gn_b, hg_norm_g, w_out)
    return y


def _rope_tables(rows):
    t = jnp.arange(rows * GRID_W)
    row = (t // GRID_W).astype(F32)
    col = (t % GRID_W).astype(F32)
    half = MLA_ROPE // 2
    inv_freq = ROPE_BASE ** (-jnp.arange(0, half, 2, dtype=F32) / half)
    ang_r = row[:, None] * inv_freq
    ang_c = col[:, None] * inv_freq
    ang = jnp.concatenate([ang_r, ang_r, ang_c, ang_c], axis=-1)
    return jnp.cos(ang), jnp.sin(ang)


def _mla_weights(w_in, w_qb, w_kvb, w_out):
    hh, half = MLA_HEADS, MLA_HEADS // 2
    w_in_ext = jnp.concatenate([w_in, _rot_half(w_in[:, MLA_Q_RANK + MLA_KV_RANK:])], axis=1)
    kvb = w_kvb.reshape(MLA_KV_RANK, half, 2, 2, MLA_NOPE)
    kv128 = jnp.stack([kvb[:, :, 0], kvb[:, :, 1, ::-1]], axis=2).reshape(MLA_KV_RANK, hh, LANES)
    kv_rows = jnp.pad(kv128, ((0, 0), (0, 0), (0, MLA_SLOT - LANES))).reshape(MLA_KV_RANK, hh * MLA_SLOT)
    rope_rows = jnp.pad(jnp.eye(MLA_ROPE, dtype=F32), ((0, 0), (LANES, MLA_SLOT - LANES - MLA_ROPE)))
    w_kv_slots = jnp.concatenate([kv_rows, jnp.tile(rope_rows, (1, hh))], axis=0).astype(BF16)
    qb = w_qb.reshape(MLA_Q_RANK, hh, MLA_QK)
    nope = qb[..., :MLA_NOPE].reshape(MLA_Q_RANK, half, 2, MLA_NOPE)
    zeros = jnp.zeros_like(nope[:, :, 0])
    nope128 = jnp.stack([jnp.concatenate([nope[:, :, 0], zeros], -1),
                         jnp.concatenate([zeros, nope[:, :, 1]], -1)], axis=2).reshape(MLA_Q_RANK, hh, LANES)
    rope = qb[..., MLA_NOPE:]
    pad_rope = lambda x: jnp.pad(x, ((0, 0), (0, 0), (0, LANES - MLA_ROPE)))
    wq_slots = jnp.concatenate([nope128, pad_rope(rope)], -1).reshape(MLA_Q_RANK, hh * MLA_SLOT)
    wr_slots = pad_rope(_rot_half(rope)).reshape(MLA_Q_RANK, hh * LANES)
    w_out_perm = w_out.reshape(half, 2, MLA_V, -1)[:, ::-1].reshape(hh * MLA_V, -1)
    return w_in_ext, w_kv_slots, wq_slots.astype(BF16), wr_slots.astype(BF16), w_out_perm


def _mla_mixer(u, b, t, n_ctx, w_in, q_norm_g, w_qb, kv_norm_g, w_kvb, w_out, cos, sin):
    w_in_ext, w_kv_slots, wq_slots, wr_slots, w_out_perm = _mla_weights(w_in, w_qb, w_kvb, w_out)
    zq, z_kv = matmul(u, w_in_ext, splits=(MLA_Q_RANK, MLA_KV_RANK + 2 * MLA_ROPE))
    ones, zeros = jnp.ones((n_ctx, MLA_ROPE), F32), jnp.zeros((n_ctx, MLA_ROPE), F32)
    cs = jnp.concatenate([jnp.concatenate([ones, cos], axis=0), jnp.concatenate([zeros, sin], axis=0)], axis=1)
    kv_slots = mla_kv(z_kv, cs, kv_norm_g, w_kv_slots, t).reshape(b, t, MLA_HEADS * MLA_SLOT)
    zq = zq.reshape(b, t, MLA_Q_RANK)
    widen = lambda x, fill: jnp.pad(x, ((0, 0), (0, LANES - MLA_ROPE)), constant_values=fill)
    o_ctx = mla_attention(zq[:, :n_ctx], kv_slots, n_ctx, widen(ones, 1.0), widen(zeros, 0.0), q_norm_g,
                          wq_slots, wr_slots)
    o_lat = mla_attention(zq[:, n_ctx:], kv_slots, t, widen(cos, 1.0), widen(sin, 0.0), q_norm_g,
                          wq_slots, wr_slots)
    o = jnp.concatenate([o_ctx, o_lat], axis=1).reshape(b * t, MLA_HEADS * MLA_V)
    return matmul(o, w_out_perm)


def kernel(x, c, ctx, c_ctx, mod_w, mod_b, ln_g, ln_b, ev_w_in, rw_mu, rw_w0, rw_w2, rw_a0, rw_a2,
           rw_g2, rw_k_k, rw_k_a, rw_r_k, rw_gn_g, rw_gn_b, hg_lb, hg_norm_g, ev_w_out, od_w_in,
           mla_q_norm_g, mla_w_qb, mla_kv_norm_g, mla_w_kvb, od_w_out, moe_w_router, moe_b_router,
           moe_w_gu, moe_b_gu, moe_w_down, moe_b_down):
    b, n_lat, dm = x.shape
    n_ctx = ctx.shape[1]
    t = n_ctx + n_lat
    cos, sin = _rope_tables(n_lat // GRID_W)
    lb = jax.nn.softmax(hg_lb.astype(F32), axis=0)
    hg_lower = jnp.cumsum(lb, axis=0) - lb[0]
    c_act = c * jax.nn.sigmoid(c)
    cc_act = c_ctx * jax.nn.sigmoid(c_ctx)
    mod_in = jnp.concatenate([c_act, cc_act[None]], axis=0)
    pad = (-mod_in.shape[0]) % 8
    mod_in = jnp.pad(mod_in, ((0, pad), (0, 0)))
    mods = jnp.stack([matmul(mod_in, mod_w[layer]) + mod_b[layer] for layer in range(DEPTH)])
    mods = mods.reshape(DEPTH * mod_in.shape[0], 1, N_MOD * dm)
    nm = functools.partial(norm_mod, mods=mods, rows_per_seq=t, n_ctx=n_ctx, n_batch=b)
    h = jnp.concatenate([ctx, x], axis=1).reshape(b * t, dm)
    assert b % MOE_PARTS == 0
    part_rows = b * t // MOE_PARTS
    u16 = nm(h, None, mod=(0, 0, 1), outs=('u16',))
    for layer in range(DEPTH):
        j = layer // 2
        if layer % 2 == 0:
            y = _even_mixer(u16, b, t, n_ctx, ev_w_in[j], rw_mu[j], rw_w0[j], rw_w2[j], rw_a0[j],
                            rw_a2[j], rw_g2[j], rw_k_k[j], rw_k_a[j], rw_r_k[j], rw_gn_g[j], rw_gn_b[j],
                            hg_lower[j], hg_norm_g[j], ev_w_out[j])
        else:
            y = _mla_mixer(u16, b, t, n_ctx, od_w_in[j], mla_q_norm_g[j], mla_w_qb[j],
                           mla_kv_norm_g[j], mla_w_kvb[j], od_w_out[j], cos, sin)
        h, u16, idx, gts = nm(h, y, gate=(layer, 2), ln=(ln_g[layer, 0], ln_b[layer, 0]),
                              mod=(layer, 3, 4), outs=('h', 'u16', 'idx', 'gates'),
                              router=(moe_w_router[layer], moe_b_router[layer]))
        f = [moe(idx, u16, layer, moe_w_gu, moe_b_gu, moe_w_down, moe_b_down, p * part_rows, part_rows)
             for p in range(MOE_PARTS)]
        fg = gts
        if layer + 1 < DEPTH:
            h, u16 = nm(h, f, gate=(layer, 5), ln=(ln_g[layer, 1], ln_b[layer, 1]),
                        mod=(layer + 1, 0, 1), outs=('h', 'u16'), slot_gates=fg)
        else:
            h = nm(h, f, gate=(layer, 5), ln=(ln_g[layer, 1], ln_b[layer, 1]), outs=('h',),
                   slot_gates=fg)
    return h.reshape(b, t, dm)[:, n_ctx:]
```

```python
import functools

import jax
import jax.numpy as jnp
from jax import lax
from jax.experimental import pallas as pl
from jax.experimental.pallas import tpu as pltpu

F32 = jnp.float32
BF16 = jnp.bfloat16
HIGHEST = lax.Precision.HIGHEST

DEPTH = 4
GRID_W = 64
N_MOD = 6

RW_HEADS = 8
RW_HEAD_DIM = 64
RW_DIM = RW_HEADS * RW_HEAD_DIM
RW_DECAY_LORA = 64
RW_ICLR_LORA = 64
RW_GATE_LORA = 128
RW_GN_EPS = 64e-5
RW_IN = 3 * RW_DIM + 2 * RW_DECAY_LORA + 2 * RW_ICLR_LORA + RW_GATE_LORA
RW_LORA_OFF = 3 * RW_DIM

HG_HEADS = 4
HG_KEY_DIM = 128
HG_DIM = HG_HEADS * HG_KEY_DIM
HG_VDIM = HG_DIM
HG_IN = 3 * HG_DIM + 2 * HG_VDIM
HG_F_FLOOR = 1e-30

MLA_HEADS = 16
MLA_NOPE = 64
MLA_ROPE = 32
MLA_V = 64
MLA_Q_RANK = 256
MLA_KV_RANK = 128
MLA_QK = MLA_NOPE + MLA_ROPE
MLA_SCALE = MLA_QK ** -0.5
ROPE_BASE = 10000.0

N_EXPERTS = 32
TOP_K = 4
EXPERT_DIM = 1024
SWIGLU_LIMIT = 7.0
SWIGLU_ALPHA = 1.702

DEEPNORM_ALPHA = (2 * DEPTH) ** 0.25
LN_EPS = 1e-5
RMS_EPS = 1e-6

LANES = 128
SCAN_CHUNK = 64
SCAN_BATCH = 2
HG_SUB = 16
MOE_TILE = 512
RANK_BLOCK = 256
MOE_PARTS = 2
VMEM_LIMIT = 56 * 1024 * 1024


def _cparams(sem):
    return pltpu.CompilerParams(dimension_semantics=sem, vmem_limit_bytes=VMEM_LIMIT)


def _row_tile(m, tm):
    if m <= tm:
        return m
    while m % tm:
        tm //= 2
    assert tm >= 8
    return tm


def _mm_kernel(x_ref, w_ref, *o_refs):
    acc = jnp.dot(x_ref[...].astype(BF16), w_ref[...], preferred_element_type=F32)
    off = 0
    for o_ref in o_refs:
        width = o_ref.shape[1]
        o_ref[...] = acc[:, off:off + width].astype(o_ref.dtype)
        off += width


def matmul(x, w, splits=None, out_dtype=F32, tm=512):
    m, k = x.shape
    n = w.shape[1]
    tm = _row_tile(m, tm)
    widths = (n,) if splits is None else splits
    assert sum(widths) == n
    outs = pl.pallas_call(
        _mm_kernel,
        grid=(m // tm,),
        in_specs=[pl.BlockSpec((tm, k), lambda i: (i, 0)),
                  pl.BlockSpec((k, n), lambda i: (0, 0))],
        out_specs=[pl.BlockSpec((tm, wd), lambda i: (i, 0)) for wd in widths],
        out_shape=[jax.ShapeDtypeStruct((m, wd), out_dtype) for wd in widths],
        compiler_params=_cparams(("parallel",)),
    )(x, w.astype(BF16))
    return outs[0] if splits is None else outs


def _route_top4(u, w, b):
    x = jnp.dot(u, w, preferred_element_type=F32, precision=HIGHEST) + b
    tm, ne = x.shape
    lane = lax.broadcasted_iota(jnp.int32, (tm, ne), 1).astype(F32)
    slot = lax.broadcasted_iota(jnp.int32, (tm, LANES), 1)
    idx = jnp.zeros((tm, LANES), F32)
    tops = []
    for k in range(TOP_K):
        mx = jnp.max(x, axis=-1, keepdims=True)
        ix = jnp.min(jnp.where(x == mx, lane, float(ne)), axis=-1, keepdims=True)
        tops.append(mx)
        idx = jnp.where(slot == k, ix, idx)
        x = jnp.where(lane == ix, -jnp.inf, x)
    e = [jnp.exp(v - tops[0]) for v in tops]
    inv = 1.0 / sum(e)
    gates = jnp.zeros((tm, LANES), F32)
    for k in range(TOP_K):
        gates = jnp.where(slot == k, e[k] * inv, gates)
    return idx.astype(jnp.int32), gates


def _bwd_chunk(c, n_ctx_chunks, n_chunks):
    return jnp.where(c < n_ctx_chunks, n_ctx_chunks - 1 - c, n_chunks - 1 - (c - n_ctx_chunks))


def _scan_streams(fwd_ref, bwd_ref):
    return [(bi, d, ref) for bi in range(SCAN_BATCH) for d, ref in ((0, fwd_ref), (1, bwd_ref))]


def _dot(a, b):
    return jnp.dot(a.astype(BF16), b.astype(BF16), preferred_element_type=F32)


def _dot_nt(a, b):
    return lax.dot_general(a.astype(BF16), b.astype(BF16), (((1,), (1,)), ((), ())),
                           preferred_element_type=F32)


def _dot_tn(a, b):
    return lax.dot_general(a.astype(BF16), b.astype(BF16), (((0,), (0,)), ((), ())),
                           preferred_element_type=F32)


def _dot_f32(a, b):
    return jnp.dot(a, b, preferred_element_type=F32, precision=HIGHEST)


def _seg_sum(x, ones_bd):
    hi = x.astype(BF16)
    lo = (x - hi.astype(F32)).astype(BF16)
    return (jnp.dot(hi, ones_bd, preferred_element_type=F32)
            + jnp.dot(lo, ones_bd, preferred_element_type=F32))


def _block_diag_ones(width, block):
    i = jnp.arange(width) // block
    return (i[:, None] == i[None, :]).astype(BF16)


def _hgrn_kernel(zf_ref, zb_ref, lo_ref, bd_ref, of_ref, ob_ref, st_ref):
    c = pl.program_id(1)
    n, sub = SCAN_CHUNK, HG_SUB
    nb = n // sub

    @pl.when(c == 0)
    def _():
        st_ref[...] = jnp.zeros_like(st_ref)

    row = lax.broadcasted_iota(jnp.int32, (n, n), 0)
    col = lax.broadcasted_iota(jnp.int32, (n, n), 1)
    colb = lax.broadcasted_iota(jnp.int32, (sub, LANES), 1)
    rown = lax.broadcasted_iota(jnp.int32, (sub, n), 0)
    coln = lax.broadcasted_iota(jnp.int32, (sub, n), 1)
    bd2 = bd_ref[...]

    dirs = []
    for bi, d, z_ref in _scan_streams(zf_ref, zb_ref):
        z = z_ref[bi]
        lower = lo_ref[d]
        qraw = z[:, :HG_DIM]
        sig = jax.nn.sigmoid(z[:, HG_DIM * (1 + d):HG_DIM * (2 + d)])
        v = z[:, 3 * HG_DIM:3 * HG_DIM + HG_VDIM]
        f = lower + (1.0 - lower) * sig
        g = jnp.log(jnp.maximum(f, HG_F_FLOOR))
        kin = (1.0 - lower) * (1.0 - sig)
        q = qraw * jax.nn.sigmoid(qraw)
        lag = (row - col) if d == 0 else (col - row)
        cum = _dot_f32((lag >= 0).astype(F32), g)
        cum_ex = cum - g
        tot = jnp.sum(g, axis=0, keepdims=True)
        blocks = []
        for j in range(nb):
            lo, hi = j * sub, (j + 1) * sub
            q_j, cum_j = q[lo:hi], cum[lo:hi]
            ref_j = cum_ex[lo:lo + 1] if d == 0 else cum_ex[hi - 1:hi]
            qt = q_j * jnp.exp(cum_j - ref_j)
            kt = kin * jnp.exp(jnp.minimum(ref_j - cum, 0.0))
            pair = jnp.concatenate(
                [q_j * kin[s:s + 1] * jnp.exp(jnp.minimum(cum_j - cum[s:s + 1], 0.0))
                 for s in range(lo, hi)], axis=0)
            blocks.append((qt, kt, pair))
        dirs.append(dict(bi=bi, d=d, v=v, qe=q * jnp.exp(cum), kdec=kin * jnp.exp(tot - cum),
                         etot=jnp.exp(tot), blocks=blocks))

    for dd in dirs:
        d = dd['d']
        for j, (qt, kt, pair) in enumerate(dd['blocks']):
            lo, hi = j * sub, (j + 1) * sub
            sums = [_dot(pair[:, 2 * LANES * hp:2 * LANES * (hp + 1)], bd2) for hp in range(HG_HEADS // 2)]
            for h in range(HG_HEADS):
                hs = slice(LANES * h, LANES * (h + 1))
                rs = sums[h // 2][:, LANES * (h % 2):LANES * (h % 2 + 1)]
                diag = jnp.zeros((sub, LANES), F32)
                for s in range(sub):
                    diag = jnp.where(colb == lo + s, rs[s * sub:(s + 1) * sub], diag)
                off = _dot_nt(qt[:, hs], kt[:, hs])
                if d == 0:
                    earlier, ordered = coln < lo, rown + lo >= coln
                else:
                    earlier, ordered = coln >= hi, rown + lo <= coln
                sc = jnp.where(earlier, off, jnp.where(ordered, diag[:, :n], 0.0))
                dd.setdefault(('out', h), []).append(_dot(sc, dd['v'][:, hs]))

    for dd in dirs:
        bi, d = dd['bi'], dd['d']
        o_ref = of_ref if d == 0 else ob_ref
        for h in range(HG_HEADS):
            hs = slice(LANES * h, LANES * (h + 1))
            slot = (bi * 2 + d) * HG_HEADS + h
            st = st_ref[slot]
            out = jnp.concatenate(dd[('out', h)], axis=0) + _dot_nt(dd['qe'][:, hs], st)
            o_ref[bi, :, hs] = out
            st_ref[slot] = (st * dd['etot'][:, hs]
                                        + _dot_tn(dd['v'][:, hs], dd['kdec'][:, hs]))


def hgrn2_scan(zb, lower, n_ctx):
    b, t, _ = zb.shape
    n, sb = SCAN_CHUNK, SCAN_BATCH
    assert b % sb == 0
    nc, ncc = t // n, n_ctx // n
    bwd = functools.partial(_bwd_chunk, n_ctx_chunks=ncc, n_chunks=nc)
    const = lambda shape: pl.BlockSpec(shape, lambda bi, c: (0,) * len(shape))
    return pl.pallas_call(
        _hgrn_kernel,
        grid=(b // sb, nc),
        in_specs=[pl.BlockSpec((sb, n, HG_IN), lambda bi, c: (bi, c, 0)),
                  pl.BlockSpec((sb, n, HG_IN), lambda bi, c: (bi, bwd(c), 0)),
                  const((2, 1, HG_DIM)), const((2 * LANES, 2 * LANES))],
        out_specs=[pl.BlockSpec((sb, n, HG_VDIM), lambda bi, c: (bi, c, 0)),
                   pl.BlockSpec((sb, n, HG_VDIM), lambda bi, c: (bi, bwd(c), 0))],
        out_shape=[jax.ShapeDtypeStruct((b, t, HG_VDIM), F32)] * 2,
        scratch_shapes=[pltpu.VMEM((sb * 2 * HG_HEADS, HG_KEY_DIM, HG_KEY_DIM), F32)],
        compiler_params=_cparams(("parallel", "arbitrary")),
    )(zb, zb, lower.reshape(2, 1, HG_DIM), _block_diag_ones(2 * LANES, LANES))


def _rwkv_iclr(z, d, a0_ref, a2_ref):
    ad = z[:, RW_LORA_OFF + LANES:RW_LORA_OFF + 2 * LANES]
    return jax.nn.sigmoid(a0_ref[d] + _dot(ad, a2_ref[d]))


def _rwkv_scan_kernel(zf_ref, zb_ref, w0_ref, w2_ref, a0_ref, a2_ref, kk_ref, ka_ref, bd_ref,
                      yf_ref, yb_ref, st_ref):
    c = pl.program_id(1)
    n, cdim = SCAN_CHUNK, RW_DIM
    npair = cdim // LANES

    @pl.when(c == 0)
    def _():
        st_ref[...] = jnp.zeros_like(st_ref)

    row = lax.broadcasted_iota(jnp.int32, (n, n), 0)
    col = lax.broadcasted_iota(jnp.int32, (n, n), 1)
    row4 = lax.broadcasted_iota(jnp.int32, (2 * n, 4 * n), 0) % n
    col4 = lax.broadcasted_iota(jnp.int32, (2 * n, 4 * n), 1) % n
    lane = lax.broadcasted_iota(jnp.int32, (1, LANES), 1)
    half = [lane < RW_HEAD_DIM, lane >= RW_HEAD_DIM]

    def stack2(x):
        return jnp.concatenate([jnp.where(half[0], x, 0.0), jnp.where(half[1], x, 0.0)], axis=0)

    r2 = lax.broadcasted_iota(jnp.int32, (LANES, LANES), 0) < RW_HEAD_DIM
    c2 = lax.broadcasted_iota(jnp.int32, (LANES, LANES), 1) < RW_HEAD_DIM
    same_head = r2 == c2

    chains = []
    for bi, d, z_ref in _scan_streams(zf_ref, zb_ref):
        z = z_ref[bi]
        r, k, v = z[:, :cdim], z[:, cdim:2 * cdim], z[:, 2 * cdim:3 * cdim]
        wd = z[:, RW_LORA_OFF:RW_LORA_OFF + LANES]
        w_pre = w0_ref[d] + _dot(jnp.tanh(wd), w2_ref[d])
        lw = -jnp.exp(-0.5) * jax.nn.sigmoid(w_pre)
        a = _rwkv_iclr(z, d, a0_ref, a2_ref)
        kk = k * kk_ref[...]
        kk = kk * lax.rsqrt(jnp.maximum(_seg_sum(kk * kk, bd_ref[...]), 1e-24))
        kd = k * (1.0 + (a - 1.0) * ka_ref[...])
        bb = kk * a
        lag = (row - col) if d == 0 else (col - row)
        lag4 = (row4 - col4) if d == 0 else (col4 - row4)
        incl4, strict4 = lag4 >= 0, lag4 > 0
        cum = _dot_f32((lag >= 0).astype(F32), lw)
        tot = jnp.sum(lw, axis=0, keepdims=True)
        g_inv = jnp.exp(-cum)
        g_tail = jnp.exp(tot - cum)
        rt = r * jnp.exp(cum)
        at = -kk * jnp.exp(cum - lw)
        etot = jnp.exp(tot)
        bt, kt = bb * g_inv, kd * g_inv
        tail = jnp.concatenate([bb * g_tail, kd * g_tail], axis=0)
        for p in range(npair):
            ps = slice(LANES * p, LANES * (p + 1))
            chains.append(dict(b=bi, d=d, p=p, ps=ps, incl4=incl4, strict4=strict4, tail=tail[:, ps],
                               at=stack2(at[:, ps]), rt=rt[:, ps], bt=bt[:, ps], kt=kt[:, ps],
                               v=v[:, ps], etot=etot[:, ps]))

    for ch in chains:
        left = jnp.concatenate([ch['at'], stack2(ch['rt'])], axis=0)
        right = jnp.concatenate([stack2(ch['bt']), stack2(ch['kt'])], axis=0)
        prod = _dot_nt(left, right)
        ch['top'] = jnp.where(ch['strict4'], prod[:2 * n], 0.0)
        ch['bot'] = jnp.where(ch['incl4'], prod[2 * n:], 0.0)
    for ch in chains:
        ch['vs'] = stack2(ch['v'])
        ch['x'] = jnp.concatenate([ch['at'], _dot(ch['top'][:, 2 * n:], ch['vs'])], axis=1)
        ch['a'] = ch['top'][:, :2 * n]
    steps = n.bit_length() - 1
    for i in range(steps):
        for ch in chains:
            ch['x'] = ch['x'] + _dot(ch['a'], ch['x'])
            if i + 1 < steps:
                ch['a'] = _dot(ch['a'], ch['a'])
    for ch in chains:
        x = ch['x'][:n] + ch['x'][n:]
        slot = (ch['b'] * 2 + ch['d']) * npair + ch['p']
        st = st_ref[slot]
        ws = _dot_nt(jnp.concatenate([x[:, :LANES], ch['rt']], axis=0), st)
        u = ws[:n] + x[:, LANES:]
        ys = _dot(ch['bot'], jnp.concatenate([stack2(u), ch['vs']], axis=0))
        y_ref = yf_ref if ch['d'] == 0 else yb_ref
        y_ref[ch['b'], :, ch['ps']] = ws[n:] + ys[:n] + ys[n:]
        delta = _dot_tn(jnp.concatenate([u, ch['v']], axis=0), ch['tail'])
        st_ref[slot] = st * ch['etot'] + jnp.where(same_head, delta, 0.0)


def _pad_lora(w, d):
    return jnp.concatenate([w[d] if i == d else jnp.zeros_like(w[i]) for i in range(2)], axis=0)


def rwkv7_scan(za, w0, w2, a0, a2, k_k, k_a, n_ctx):
    b, t, _ = za.shape
    n, cdim, sb = SCAN_CHUNK, RW_DIM, SCAN_BATCH
    assert b % sb == 0
    nc, ncc = t // n, n_ctx // n
    bwd = functools.partial(_bwd_chunk, n_ctx_chunks=ncc, n_chunks=nc)
    const = lambda shape: pl.BlockSpec(shape, lambda bi, c: (0,) * len(shape))
    w2p = jnp.stack([_pad_lora(w2, d) for d in range(2)]).astype(BF16)
    a2p = jnp.stack([_pad_lora(a2, d) for d in range(2)]).astype(BF16)
    return pl.pallas_call(
        _rwkv_scan_kernel,
        grid=(b // sb, nc),
        in_specs=[pl.BlockSpec((sb, n, RW_IN), lambda bi, c: (bi, c, 0)),
                  pl.BlockSpec((sb, n, RW_IN), lambda bi, c: (bi, bwd(c), 0)),
                  const((2, 1, cdim)), const((2, LANES, cdim)), const((2, 1, cdim)),
                  const((2, LANES, cdim)), const((1, cdim)), const((1, cdim)), const((cdim, cdim))],
        out_specs=[pl.BlockSpec((sb, n, cdim), lambda bi, c: (bi, c, 0)),
                   pl.BlockSpec((sb, n, cdim), lambda bi, c: (bi, bwd(c), 0))],
        out_shape=[jax.ShapeDtypeStruct((b, t, cdim), F32)] * 2,
        scratch_shapes=[pltpu.VMEM((sb * 2 * cdim // LANES, LANES, LANES), F32)],
        compiler_params=_cparams(("parallel", "arbitrary")),
    )(za, za, w0.reshape(2, 1, cdim), w2p, a0.reshape(2, 1, cdim), a2p, k_k.reshape(1, cdim),
      k_a.reshape(1, cdim), _block_diag_ones(cdim, RW_HEAD_DIM))


def _even_post_kernel(za_ref, yf_ref, yb_ref, zg_ref, of_ref, ob_ref, a0_ref, a2_ref, g2_ref,
                      ka_ref, rk_ref, gng_ref, gnb_ref, hgg_ref, bd64_ref, bd128_ref, wout_ref,
                      o_ref):
    cdim = RW_DIM
    z = za_ref[...]
    r, k, v = z[:, :cdim], z[:, cdim:2 * cdim], z[:, 2 * cdim:3 * cdim]
    a_sum = _rwkv_iclr(z, 0, a0_ref, a2_ref) + _rwkv_iclr(z, 1, a0_ref, a2_ref)
    k_sum = k * (2.0 + (a_sum - 2.0) * ka_ref[...])
    gate = _dot(jax.nn.sigmoid(z[:, RW_LORA_OFF + 2 * LANES:]), g2_ref[...])
    bd64 = bd64_ref[...]
    y = yf_ref[...] + yb_ref[...]
    mean = _seg_sum(y, bd64) * (1.0 / RW_HEAD_DIM)
    cen = y - mean
    var = _seg_sum(cen * cen, bd64) * (1.0 / RW_HEAD_DIM)
    y = cen * lax.rsqrt(var + RW_GN_EPS) * gng_ref[...] + gnb_ref[...]
    bonus = _seg_sum(r * k_sum * rk_ref[...], bd64) * v
    y_rw = (y + bonus) * gate

    o = of_ref[...] + ob_ref[...]
    ms = _seg_sum(o * o, bd128_ref[...]) * (1.0 / HG_KEY_DIM)
    g_hg = zg_ref[...]
    y_hg = o * lax.rsqrt(ms + RMS_EPS) * hgg_ref[...] * (g_hg * jax.nn.sigmoid(g_hg))

    y_all = jnp.concatenate([y_rw, y_hg], axis=1).astype(BF16)
    o_ref[...] = jnp.dot(y_all, wout_ref[...], preferred_element_type=F32)


def even_post(za, yf, yb, zb, of, ob, a0, a2, g2, k_a, r_k, gn_g, gn_b, hg_norm_g, w_out, tm=256):
    m = za.shape[0]
    cdim, dm = RW_DIM, w_out.shape[1]
    tm = _row_tile(m, tm)
    rows = lambda width, blk=0: pl.BlockSpec((tm, width), lambda i: (i, blk))
    const = lambda shape: pl.BlockSpec(shape, lambda i: (0,) * len(shape))
    a2p = jnp.stack([_pad_lora(a2, d) for d in range(2)]).astype(BF16)
    vec = lambda x: x.reshape(1, cdim)
    return pl.pallas_call(
        _even_post_kernel,
        grid=(m // tm,),
        in_specs=[rows(RW_IN), rows(cdim), rows(cdim), rows(HG_VDIM, (HG_IN - HG_VDIM) // HG_VDIM),
                  rows(HG_VDIM), rows(HG_VDIM),
                  const((2, 1, cdim)), const((2, LANES, cdim)), const((RW_GATE_LORA, cdim)),
                  const((1, cdim)), const((1, cdim)), const((1, cdim)), const((1, cdim)),
                  const((1, HG_VDIM)), const((cdim, cdim)), const((HG_VDIM, HG_VDIM)),
                  const((cdim + HG_VDIM, dm))],
        out_specs=rows(dm),
        out_shape=jax.ShapeDtypeStruct((m, dm), F32),
        compiler_params=_cparams(("parallel",)),
    )(za, yf, yb, zb, of, ob, a0.reshape(2, 1, cdim), a2p, g2.astype(BF16), vec(k_a), vec(r_k),
      vec(gn_g), vec(gn_b), hg_norm_g.reshape(1, HG_VDIM), _block_diag_ones(cdim, RW_HEAD_DIM),
      _block_diag_ones(HG_VDIM, HG_KEY_DIM), w_out.astype(BF16))


MLA_SLOT = 2 * LANES
MLA_GROUP = 4


def _rot_half(w):
    r1, r2, c1, c2 = jnp.split(w, 4, axis=-1)
    return jnp.concatenate([-r2, r1, -c2, c1], axis=-1)


def _mla_kv_kernel(z_ref, cs_ref, g_ref, w_ref, o_ref):
    z = z_ref[...]
    kva = z[:, :MLA_KV_RANK]
    kva = kva * lax.rsqrt(jnp.mean(kva * kva, axis=-1, keepdims=True) + RMS_EPS) * g_ref[...]
    cs = cs_ref[...]
    kpe = (z[:, MLA_KV_RANK:MLA_KV_RANK + MLA_ROPE] * cs[:, :MLA_ROPE]
           + z[:, MLA_KV_RANK + MLA_ROPE:] * cs[:, MLA_ROPE:])
    x = jnp.concatenate([kva, kpe], axis=1).astype(BF16)
    o_ref[...] = jnp.dot(x, w_ref[...], preferred_element_type=F32).astype(o_ref.dtype)


def mla_kv(z_kv, cs, kv_norm_g, w_kv_slots, rows_per_seq, tm=256):
    m, width = z_kv.shape
    tm = _row_tile(rows_per_seq, tm)
    per_seq = rows_per_seq // tm
    n_out = w_kv_slots.shape[1]
    return pl.pallas_call(
        _mla_kv_kernel,
        grid=(m // tm,),
        in_specs=[pl.BlockSpec((tm, width), lambda i: (i, 0)),
                  pl.BlockSpec((tm, 2 * MLA_ROPE), lambda i: (i % per_seq, 0)),
                  pl.BlockSpec((1, MLA_KV_RANK), lambda i: (0, 0)),
                  pl.BlockSpec(w_kv_slots.shape, lambda i: (0, 0))],
        out_specs=pl.BlockSpec((tm, n_out), lambda i: (i, 0)),
        out_shape=jax.ShapeDtypeStruct((m, n_out), BF16),
        compiler_params=_cparams(("parallel",)),
    )(z_kv, cs, kv_norm_g.reshape(1, MLA_KV_RANK), w_kv_slots)


def _mla_attn_kernel(zq_ref, kv_ref, cos_ref, sin_ref, g_ref, wq_ref, wr_ref, o_ref):
    zq = zq_ref[0]
    zn = (zq * lax.rsqrt(jnp.mean(zq * zq, axis=-1, keepdims=True) + RMS_EPS) * g_ref[...]).astype(BF16)
    a = jnp.dot(zn, wq_ref[...], preferred_element_type=F32)
    ar = jnp.dot(zn, wr_ref[...], preferred_element_type=F32)
    cos, sin = cos_ref[...], sin_ref[...]
    low = lax.broadcasted_iota(jnp.int32, (1, LANES), 1) < MLA_V
    blocks = []
    for pair in range(MLA_GROUP // 2):
        res = []
        for j in range(2):
            h = 2 * pair + j
            q_nope = a[:, MLA_SLOT * h:MLA_SLOT * h + LANES]
            q_rope = (a[:, MLA_SLOT * h + LANES:MLA_SLOT * (h + 1)] * cos
                      + ar[:, LANES * h:LANES * (h + 1)] * sin)
            qf = (jnp.concatenate([q_nope, q_rope], axis=1) * MLA_SCALE).astype(BF16)
            s = lax.dot_general(qf, kv_ref[0, :, MLA_SLOT * h:MLA_SLOT * (h + 1)],
                                (((1,), (1,)), ((), ())), preferred_element_type=F32)
            p = jnp.exp(s - jnp.max(s, axis=-1, keepdims=True))
            l = jnp.sum(p, axis=-1, keepdims=True)
            res.append(jnp.dot(p.astype(BF16), kv_ref[0, :, MLA_SLOT * h:MLA_SLOT * h + LANES],
                               preferred_element_type=F32) / l)
        blocks.append(jnp.where(low, res[1], res[0]))
    o_ref[0] = jnp.concatenate(blocks, axis=1).astype(o_ref.dtype)


def mla_attention(zq, kv_slots, n_keys, cos, sin, q_norm_g, wq_slots, wr_slots, tq=1024):
    b, nq, _ = zq.shape
    tq = _row_tile(nq, tq)
    gw = MLA_GROUP * MLA_SLOT
    return pl.pallas_call(
        _mla_attn_kernel,
        grid=(b, MLA_HEADS // MLA_GROUP, nq // tq),
        in_specs=[pl.BlockSpec((1, tq, MLA_Q_RANK), lambda bi, g, qi: (bi, qi, 0)),
                  pl.BlockSpec((1, n_keys, gw), lambda bi, g, qi: (bi, 0, g)),
                  pl.BlockSpec((tq, LANES), lambda bi, g, qi: (qi, 0)),
                  pl.BlockSpec((tq, LANES), lambda bi, g, qi: (qi, 0)),
                  pl.BlockSpec((1, MLA_Q_RANK), lambda bi, g, qi: (0, 0)),
                  pl.BlockSpec((MLA_Q_RANK, gw), lambda bi, g, qi: (0, g)),
                  pl.BlockSpec((MLA_Q_RANK, gw // 2), lambda bi, g, qi: (0, g))],
        out_specs=pl.BlockSpec((1, tq, MLA_GROUP * MLA_V), lambda bi, g, qi: (bi, qi, g)),
        out_shape=jax.ShapeDtypeStruct((b, nq, MLA_HEADS * MLA_V), BF16),
        compiler_params=_cparams(("parallel", "parallel", "parallel")),
    )(zq, kv_slots, cos, sin, q_norm_g.reshape(1, MLA_Q_RANK), wq_slots, wr_slots)


def _moe_kernel(te_ref, nt_ref, x_ref, wgu_ref, bgu_ref, wd_ref, bd_ref, o_ref, wgu_bf, wd_bf):
    i = pl.program_id(0)
    live = i < nt_ref[0]
    new_expert = jnp.logical_or(i == 0, te_ref[i] != te_ref[jnp.maximum(i - 1, 0)])

    @pl.when(jnp.logical_and(live, new_expert))
    def _():
        wgu_bf[...] = wgu_ref[0, 0].astype(BF16)
        wd_bf[...] = wd_ref[0, 0].astype(BF16)

    @pl.when(live)
    def _():
        h = jnp.dot(x_ref[...], wgu_bf[...], preferred_element_type=F32) + bgu_ref[0, 0]
        glu = jnp.minimum(h[:, :EXPERT_DIM], SWIGLU_LIMIT)
        lin = jnp.clip(h[:, EXPERT_DIM:], -SWIGLU_LIMIT, SWIGLU_LIMIT)
        act = glu * jax.nn.sigmoid(SWIGLU_ALPHA * glu) * (lin + 1.0)
        y = jnp.dot(act.astype(BF16), wd_bf[...], preferred_element_type=F32) + bd_ref[0, 0]
        o_ref[...] = y.astype(o_ref.dtype)

    @pl.when(i >= nt_ref[0])
    def _():
        o_ref[...] = jnp.zeros_like(o_ref)


def moe_grouped(x_sorted, tile_expert, n_tiles_used, layer, w_gu, b_gu, w_down, b_down):
    p, dm = x_sorted.shape
    tm = MOE_TILE
    nl, e, _, f2 = w_gu.shape
    grid_spec = pltpu.PrefetchScalarGridSpec(
        num_scalar_prefetch=2,
        grid=(p // tm,),
        in_specs=[
            pl.BlockSpec((tm, dm), lambda i, te, nt: (i, 0)),
            pl.BlockSpec((1, 1, dm, f2), lambda i, te, nt: (layer, te[i], 0, 0)),
            pl.BlockSpec((1, 1, 1, f2), lambda i, te, nt: (layer, te[i], 0, 0)),
            pl.BlockSpec((1, 1, f2 // 2, dm), lambda i, te, nt: (layer, te[i], 0, 0)),
            pl.BlockSpec((1, 1, 1, dm), lambda i, te, nt: (layer, te[i], 0, 0)),
        ],
        out_specs=pl.BlockSpec((tm, dm), lambda i, te, nt: (i, 0)),
        scratch_shapes=[pltpu.VMEM((dm, f2), BF16), pltpu.VMEM((f2 // 2, dm), BF16)],
    )
    return pl.pallas_call(
        _moe_kernel,
        grid_spec=grid_spec,
        out_shape=jax.ShapeDtypeStruct((p, dm), F32),
        compiler_params=_cparams(("arbitrary",)),
    )(tile_expert, n_tiles_used, x_sorted, w_gu, b_gu.reshape(nl, e, 1, f2), w_down,
      b_down.reshape(nl, e, 1, dm))


def moe(idx_lanes, u16, layer, w_gu, b_gu, w_down, b_down, row0, n):
    dm = u16.shape[1]
    tm = MOE_TILE
    top_idx = idx_lanes[row0:row0 + n, :TOP_K]
    e_flat = top_idx.T.reshape(-1).astype(jnp.int32)
    npair = n * TOP_K
    rb = _row_tile(npair, RANK_BLOCK)
    onehot = (e_flat[:, None] == jnp.arange(N_EXPERTS, dtype=jnp.int32)[None, :])
    oh3 = onehot.astype(BF16).reshape(npair // rb, rb, N_EXPERTS)
    earlier = (jnp.arange(rb)[:, None] > jnp.arange(rb)[None, :]).astype(BF16)
    within = jnp.einsum('ij,bje->bie', earlier, oh3, preferred_element_type=F32)
    blk_tot = jnp.sum(oh3.astype(F32), axis=1)
    blk_off = jnp.cumsum(blk_tot, axis=0) - blk_tot
    counts = jnp.sum(blk_tot, axis=0).astype(jnp.int32)
    padded = ((counts + tm - 1) // tm) * tm
    ends_p = jnp.cumsum(padded)
    starts_p = ends_p - padded
    starts = jnp.cumsum(counts) - counts
    rank = within + blk_off[:, None, :] + starts_p.astype(F32)[None, None, :]
    pos = jnp.sum(oh3.astype(F32) * rank, axis=-1).astype(jnp.int32).reshape(npair)
    p_rows = npair + N_EXPERTS * tm
    n_tiles = p_rows // tm
    tile_start = jnp.arange(n_tiles, dtype=jnp.int32) * tm
    tile_expert = jnp.minimum(jnp.sum((ends_p[None, :] <= tile_start[:, None]).astype(jnp.int32), axis=1),
                              N_EXPERTS - 1)
    n_used = (ends_p[-1] // tm).astype(jnp.int32).reshape(1)
    _, sorted_tok = lax.sort_key_val(e_flat, jnp.arange(npair, dtype=jnp.int32) % n)
    onehot_t = (tile_expert[:, None] == jnp.arange(N_EXPERTS, dtype=jnp.int32)[None, :]).astype(jnp.int32)
    shift = jnp.sum(onehot_t * (starts - starts_p)[None, :], axis=1)
    last = jnp.sum(onehot_t * (starts + counts)[None, :], axis=1)
    rows = jnp.arange(p_rows, dtype=jnp.int32).reshape(n_tiles, tm)
    src = rows + shift[:, None]
    valid = src < last[:, None]
    src_tok = jnp.where(valid, sorted_tok[jnp.where(valid, src, rows % npair).reshape(-1)].reshape(n_tiles, tm),
                        rows % n).reshape(-1)
    x_sorted = u16[src_tok + row0]
    y_sorted = moe_grouped(x_sorted, tile_expert, n_used, layer, w_gu, b_gu, w_down, b_down)
    return y_sorted[pos].reshape(TOP_K, n, dm)


def _norm_mod_kernel(*refs, dm, gate, mod, has_norm, combine, tiles_per_part, route, outs):
    refs = list(refs)
    h = refs.pop(0)[...]
    if has_norm:
        if combine:
            parts = [refs.pop(0) for _ in range(combine)]
            slot_gate = refs.pop(0)[...]
            part = pl.program_id(0) // tiles_per_part
            y = None
            for p, y_ref in enumerate(parts):
                yp = sum(y_ref[k] * slot_gate[:, k:k + 1] for k in range(TOP_K))
                y = yp if y is None else jnp.where(part == p, yp, y)
        else:
            y = refs.pop(0)[...]
        mg = refs.pop(0)
        lng, lnb = refs.pop(0)[...], refs.pop(0)[...]
        x = DEEPNORM_ALPHA * h + mg[0, :, gate * dm:(gate + 1) * dm] * y
        mu = jnp.mean(x, axis=-1, keepdims=True)
        cen = x - mu
        var = jnp.mean(cen * cen, axis=-1, keepdims=True)
        h = cen * lax.rsqrt(var + LN_EPS) * lng + lnb
    if mod is not None:
        mm = refs.pop(0)
        shift, scale = mod
        u = h * (1.0 + mm[0, :, scale * dm:(scale + 1) * dm]) + mm[0, :, shift * dm:(shift + 1) * dm]
    vals = {'h': h}
    if mod is not None:
        vals['u16'] = u
    if route:
        wr, br = refs.pop(0)[...], refs.pop(0)[...]
        vals['idx'], vals['gates'] = _route_top4(u, wr, br)
    for kind, o_ref in zip(outs, refs):
        o_ref[...] = vals[kind].astype(o_ref.dtype)


def norm_mod(h, y, mods, rows_per_seq, n_ctx, n_batch, gate=None, ln=None, mod=None, outs=('h',),
             slot_gates=None, router=None, tm=256):
    m, dm = h.shape
    tm = _row_tile(n_ctx, tm)
    per_seq, ctx_tiles = rows_per_seq // tm, n_ctx // tm
    rows = mods.shape[0] // DEPTH

    def mod_row(layer):
        return lambda i: (layer * rows + jnp.where(i % per_seq < ctx_tiles, n_batch, i // per_seq), 0, 0)

    tile = pl.BlockSpec((tm, dm), lambda i: (i, 0))
    vec = pl.BlockSpec((1, dm), lambda i: (0, 0))
    args, specs = [h], [tile]
    tiles_per_part = 0
    if gate is not None:
        if slot_gates is None:
            args.append(y)
            specs.append(tile)
        else:
            tiles_per_part = m // tm // len(y)
            for p, part in enumerate(y):
                args.append(part)
                specs.append(pl.BlockSpec(
                    (TOP_K, tm, dm),
                    lambda i, p=p: (0, jnp.clip(i - p * tiles_per_part, 0, tiles_per_part - 1), 0)))
            args.append(slot_gates)
            specs.append(pl.BlockSpec((tm, LANES), lambda i: (i, 0)))
        args += [mods, ln[0].reshape(1, dm), ln[1].reshape(1, dm)]
        specs += [pl.BlockSpec((1, 1, N_MOD * dm), mod_row(gate[0])), vec, vec]
    if mod is not None:
        args.append(mods)
        specs.append(pl.BlockSpec((1, 1, N_MOD * dm), mod_row(mod[0])))
    if router is not None:
        w_router, b_router = router
        args += [w_router, b_router.reshape(1, N_EXPERTS)]
        specs += [pl.BlockSpec((dm, N_EXPERTS), lambda i: (0, 0)), pl.BlockSpec((1, N_EXPERTS), lambda i: (0, 0))]
    dtypes = {'h': (dm, F32), 'u16': (dm, BF16), 'idx': (LANES, jnp.int32), 'gates': (LANES, F32)}
    res = pl.pallas_call(
        functools.partial(_norm_mod_kernel, dm=dm, gate=None if gate is None else gate[1],
                          mod=None if mod is None else mod[1:], has_norm=gate is not None,
                          combine=0 if slot_gates is None else len(y), tiles_per_part=tiles_per_part,
                          route=router is not None, outs=outs),
        grid=(m // tm,),
        in_specs=specs,
        out_specs=[pl.BlockSpec((tm, dtypes[k][0]), lambda i: (i, 0)) for k in outs],
        out_shape=[jax.ShapeDtypeStruct((m,) + dtypes[k][:1], dtypes[k][1]) for k in outs],
        compiler_params=_cparams(("parallel",)),
    )(*args)
    return res[0] if len(outs) == 1 else res


SUBLANES = 8


def _shift_kernel(z_ref, prev_ref, next_ref, mu_ref, o_ref, *, per_seq, ctx_tiles):
    j = pl.program_id(0) % per_seq
    z = z_ref[...]
    tm = z.shape[0]
    starts_part = jnp.logical_or(j == 0, j == ctx_tiles)
    ends_part = jnp.logical_or(j == ctx_tiles - 1, j == per_seq - 1)
    before = prev_ref[SUBLANES - 1:SUBLANES, :] * jnp.where(starts_part, 0.0, 1.0)
    after = next_ref[0:1, :] * jnp.where(ends_part, 0.0, 1.0)
    row = lax.broadcasted_iota(jnp.int32, (tm, 1), 0)
    prev = jnp.where(row == 0, before, pltpu.roll(z, 1, axis=0))
    nxt = jnp.where(row == tm - 1, after, pltpu.roll(z, tm - 1, axis=0))
    o_ref[...] = z + mu_ref[...] * (0.5 * (prev + nxt) - z)


def token_shift(z, mu, rows_per_seq, n_ctx, tm=256):
    m, width = z.shape
    tm = _row_tile(n_ctx, tm)
    per_tile = tm // SUBLANES
    last = m // SUBLANES - 1
    return pl.pallas_call(
        functools.partial(_shift_kernel, per_seq=rows_per_seq // tm, ctx_tiles=n_ctx // tm),
        grid=(m // tm,),
        in_specs=[pl.BlockSpec((tm, width), lambda i: (i, 0)),
                  pl.BlockSpec((SUBLANES, width), lambda i: (jnp.maximum(i * per_tile - 1, 0), 0)),
                  pl.BlockSpec((SUBLANES, width), lambda i: (jnp.minimum((i + 1) * per_tile, last), 0)),
                  pl.BlockSpec((1, width), lambda i: (0, 0))],
        out_specs=pl.BlockSpec((tm, width), lambda i: (i, 0)),
        out_shape=jax.ShapeDtypeStruct((m, width), F32),
        compiler_params=_cparams(("parallel",)),
    )(z, z, z, mu.reshape(1, width))


def _even_mixer(u, b, t, n_ctx, w_in, mu, w0, w2, a0, a2, g2, k_k, k_a, r_k, gn_g, gn_b, lower,
                hg_norm_g, w_out):
    za, zb = matmul(u, w_in, splits=(RW_IN, HG_IN))
    za = token_shift(za, mu, t, n_ctx).reshape(b, t, RW_IN)
    zb = zb.reshape(b, t, HG_IN)
    yf, yb = rwkv7_scan(za, w0, w2, a0, a2, k_k, k_a, n_ctx)
    of, ob = hgrn2_scan(zb, lower, n_ctx)
    flat = lambda x: x.reshape(b * t, x.shape[-1])
    y = even_post(flat(za), flat(yf), flat(yb), flat(zb), flat(of), flat(ob), a0, a2, g2, k_a,
                  r_k.reshape(-1), gn_g, gn_b, hg_norm_g, w_out)
    return y


def _rope_tables(rows):
    t = jnp.arange(rows * GRID_W)
    row = (t // GRID_W).astype(F32)
    col = (t % GRID_W).astype(F32)
    half = MLA_ROPE // 2
    inv_freq = ROPE_BASE ** (-jnp.arange(0, half, 2, dtype=F32) / half)
    ang_r = row[:, None] * inv_freq
    ang_c = col[:, None] * inv_freq
    ang = jnp.concatenate([ang_r, ang_r, ang_c, ang_c], axis=-1)
    return jnp.cos(ang), jnp.sin(ang)


def _mla_weights(w_in, w_qb, w_kvb, w_out):
    hh, half = MLA_HEADS, MLA_HEADS // 2
    w_in_ext = jnp.concatenate([w_in, _rot_half(w_in[:, MLA_Q_RANK + MLA_KV_RANK:])], axis=1)
    kvb = w_kvb.reshape(MLA_KV_RANK, half, 2, 2, MLA_NOPE)
    kv128 = jnp.stack([kvb[:, :, 0], kvb[:, :, 1, ::-1]], axis=2).reshape(MLA_KV_RANK, hh, LANES)
    kv_rows = jnp.pad(kv128, ((0, 0), (0, 0), (0, MLA_SLOT - LANES))).reshape(MLA_KV_RANK, hh * MLA_SLOT)
    rope_rows = jnp.pad(jnp.eye(MLA_ROPE, dtype=F32), ((0, 0), (LANES, MLA_SLOT - LANES - MLA_ROPE)))
    w_kv_slots = jnp.concatenate([kv_rows, jnp.tile(rope_rows, (1, hh))], axis=0).astype(BF16)
    qb = w_qb.reshape(MLA_Q_RANK, hh, MLA_QK)
    nope = qb[..., :MLA_NOPE].reshape(MLA_Q_RANK, half, 2, MLA_NOPE)
    zeros = jnp.zeros_like(nope[:, :, 0])
    nope128 = jnp.stack([jnp.concatenate([nope[:, :, 0], zeros], -1),
                         jnp.concatenate([zeros, nope[:, :, 1]], -1)], axis=2).reshape(MLA_Q_RANK, hh, LANES)
    rope = qb[..., MLA_NOPE:]
    pad_rope = lambda x: jnp.pad(x, ((0, 0), (0, 0), (0, LANES - MLA_ROPE)))
    wq_slots = jnp.concatenate([nope128, pad_rope(rope)], -1).reshape(MLA_Q_RANK, hh * MLA_SLOT)
    wr_slots = pad_rope(_rot_half(rope)).reshape(MLA_Q_RANK, hh * LANES)
    w_out_perm = w_out.reshape(half, 2, MLA_V, -1)[:, ::-1].reshape(hh * MLA_V, -1)
    return w_in_ext, w_kv_slots, wq_slots.astype(BF16), wr_slots.astype(BF16), w_out_perm


def _mla_mixer(u, b, t, n_ctx, w_in, q_norm_g, w_qb, kv_norm_g, w_kvb, w_out, cos, sin):
    w_in_ext, w_kv_slots, wq_slots, wr_slots, w_out_perm = _mla_weights(w_in, w_qb, w_kvb, w_out)
    zq, z_kv = matmul(u, w_in_ext, splits=(MLA_Q_RANK, MLA_KV_RANK + 2 * MLA_ROPE))
    ones, zeros = jnp.ones((n_ctx, MLA_ROPE), F32), jnp.zeros((n_ctx, MLA_ROPE), F32)
    cs = jnp.concatenate([jnp.concatenate([ones, cos], axis=0), jnp.concatenate([zeros, sin], axis=0)], axis=1)
    kv_slots = mla_kv(z_kv, cs, kv_norm_g, w_kv_slots, t).reshape(b, t, MLA_HEADS * MLA_SLOT)
    zq = zq.reshape(b, t, MLA_Q_RANK)
    widen = lambda x, fill: jnp.pad(x, ((0, 0), (0, LANES - MLA_ROPE)), constant_values=fill)
    o_ctx = mla_attention(zq[:, :n_ctx], kv_slots, n_ctx, widen(ones, 1.0), widen(zeros, 0.0), q_norm_g,
                          wq_slots, wr_slots)
    o_lat = mla_attention(zq[:, n_ctx:], kv_slots, t, widen(cos, 1.0), widen(sin, 0.0), q_norm_g,
                          wq_slots, wr_slots)
    o = jnp.concatenate([o_ctx, o_lat], axis=1).reshape(b * t, MLA_HEADS * MLA_V)
    return matmul(o, w_out_perm)


def kernel(x, c, ctx, c_ctx, mod_w, mod_b, ln_g, ln_b, ev_w_in, rw_mu, rw_w0, rw_w2, rw_a0, rw_a2,
           rw_g2, rw_k_k, rw_k_a, rw_r_k, rw_gn_g, rw_gn_b, hg_lb, hg_norm_g, ev_w_out, od_w_in,
           mla_q_norm_g, mla_w_qb, mla_kv_norm_g, mla_w_kvb, od_w_out, moe_w_router, moe_b_router,
           moe_w_gu, moe_b_gu, moe_w_down, moe_b_down):
    b, n_lat, dm = x.shape
    n_ctx = ctx.shape[1]
    t = n_ctx + n_lat
    cos, sin = _rope_tables(n_lat // GRID_W)
    lb = jax.nn.softmax(hg_lb.astype(F32), axis=0)
    hg_lower = jnp.cumsum(lb, axis=0) - lb[0]
    c_act = c * jax.nn.sigmoid(c)
    cc_act = c_ctx * jax.nn.sigmoid(c_ctx)
    mod_in = jnp.concatenate([c_act, cc_act[None]], axis=0)
    pad = (-mod_in.shape[0]) % 8
    mod_in = jnp.pad(mod_in, ((0, pad), (0, 0)))
    mods = jnp.stack([matmul(mod_in, mod_w[layer]) + mod_b[layer] for layer in range(DEPTH)])
    mods = mods.reshape(DEPTH * mod_in.shape[0], 1, N_MOD * dm)
    nm = functools.partial(norm_mod, mods=mods, rows_per_seq=t, n_ctx=n_ctx, n_batch=b)
    h = jnp.concatenate([ctx, x], axis=1).reshape(b * t, dm)
    assert b % MOE_PARTS == 0
    part_rows = b * t // MOE_PARTS
    u16 = nm(h, None, mod=(0, 0, 1), outs=('u16',))
    for layer in range(DEPTH):
        j = layer // 2
        if layer % 2 == 0:
            y = _even_mixer(u16, b, t, n_ctx, ev_w_in[j], rw_mu[j], rw_w0[j], rw_w2[j], rw_a0[j],
                            rw_a2[j], rw_g2[j], rw_k_k[j], rw_k_a[j], rw_r_k[j], rw_gn_g[j], rw_gn_b[j],
                            hg_lower[j], hg_norm_g[j], ev_w_out[j])
        else:
            y = _mla_mixer(u16, b, t, n_ctx, od_w_in[j], mla_q_norm_g[j], mla_w_qb[j],
                           mla_kv_norm_g[j], mla_w_kvb[j], od_w_out[j], cos, sin)
        h, u16, idx, gts = nm(h, y, gate=(layer, 2), ln=(ln_g[layer, 0], ln_b[layer, 0]),
                              mod=(layer, 3, 4), outs=('h', 'u16', 'idx', 'gates'),
                              router=(moe_w_router[layer], moe_b_router[layer]))
        f = [moe(idx, u16, layer, moe_w_gu, moe_b_gu, moe_w_down, moe_b_down, p * part_rows, part_rows)
             for p in range(MOE_PARTS)]
        fg = gts
        if layer + 1 < DEPTH:
            h, u16 = nm(h, f, gate=(layer, 5), ln=(ln_g[layer, 1], ln_b[layer, 1]),
                        mod=(layer + 1, 0, 1), outs=('h', 'u16'), slot_gates=fg)
        else:
            h = nm(h, f, gate=(layer, 5), ln=(ln_g[layer, 1], ln_b[layer, 1]), outs=('h',),
                   slot_gates=fg)
    return h.reshape(b, t, dm)[:, n_ctx:]
```

```python
import functools

import jax
import jax.numpy as jnp
from jax import lax
from jax.experimental import pallas as pl
from jax.experimental.pallas import tpu as pltpu

F32 = jnp.float32
BF16 = jnp.bfloat16
HIGHEST = lax.Precision.HIGHEST

DEPTH = 4
GRID_W = 64
N_MOD = 6

RW_HEADS = 8
RW_HEAD_DIM = 64
RW_DIM = RW_HEADS * RW_HEAD_DIM
RW_DECAY_LORA = 64
RW_ICLR_LORA = 64
RW_GATE_LORA = 128
RW_GN_EPS = 64e-5
RW_IN = 3 * RW_DIM + 2 * RW_DECAY_LORA + 2 * RW_ICLR_LORA + RW_GATE_LORA
RW_LORA_OFF = 3 * RW_DIM

HG_HEADS = 4
HG_KEY_DIM = 128
HG_DIM = HG_HEADS * HG_KEY_DIM
HG_VDIM = HG_DIM
HG_IN = 3 * HG_DIM + 2 * HG_VDIM
HG_F_FLOOR = 1e-30

MLA_HEADS = 16
MLA_NOPE = 64
MLA_ROPE = 32
MLA_V = 64
MLA_Q_RANK = 256
MLA_KV_RANK = 128
MLA_QK = MLA_NOPE + MLA_ROPE
MLA_SCALE = MLA_QK ** -0.5
ROPE_BASE = 10000.0

N_EXPERTS = 32
TOP_K = 4
EXPERT_DIM = 1024
SWIGLU_LIMIT = 7.0
SWIGLU_ALPHA = 1.702

DEEPNORM_ALPHA = (2 * DEPTH) ** 0.25
LN_EPS = 1e-5
RMS_EPS = 1e-6

LANES = 128
SCAN_CHUNK = 64
SCAN_BATCH = 2
HG_SUB = 16
MOE_TILE = 512
RANK_BLOCK = 256
MOE_PARTS = 2
VMEM_LIMIT = 56 * 1024 * 1024


def _cparams(sem):
    return pltpu.CompilerParams(dimension_semantics=sem, vmem_limit_bytes=VMEM_LIMIT)


def _row_tile(m, tm):
    if m <= tm:
        return m
    while m % tm:
        tm //= 2
    assert tm >= 8
    return tm


def _mm_kernel(x_ref, w_ref, *o_refs):
    acc = jnp.dot(x_ref[...].astype(BF16), w_ref[...], preferred_element_type=F32)
    off = 0
    for o_ref in o_refs:
        width = o_ref.shape[1]
        o_ref[...] = acc[:, off:off + width].astype(o_ref.dtype)
        off += width


def matmul(x, w, splits=None, out_dtype=F32, tm=512):
    m, k = x.shape
    n = w.shape[1]
    tm = _row_tile(m, tm)
    widths = (n,) if splits is None else splits
    assert sum(widths) == n
    outs = pl.pallas_call(
        _mm_kernel,
        grid=(m // tm,),
        in_specs=[pl.BlockSpec((tm, k), lambda i: (i, 0)),
                  pl.BlockSpec((k, n), lambda i: (0, 0))],
        out_specs=[pl.BlockSpec((tm, wd), lambda i: (i, 0)) for wd in widths],
        out_shape=[jax.ShapeDtypeStruct((m, wd), out_dtype) for wd in widths],
        compiler_params=_cparams(("parallel",)),
    )(x, w.astype(BF16))
    return outs[0] if splits is None else outs


def _route_top4(u, w, b):
    x = jnp.dot(u, w, preferred_element_type=F32, precision=HIGHEST) + b
    tm, ne = x.shape
    lane = lax.broadcasted_iota(jnp.int32, (tm, ne), 1).astype(F32)
    slot = lax.broadcasted_iota(jnp.int32, (tm, LANES), 1)
    idx = jnp.zeros((tm, LANES), F32)
    tops = []
    for k in range(TOP_K):
        mx = jnp.max(x, axis=-1, keepdims=True)
        ix = jnp.min(jnp.where(x == mx, lane, float(ne)), axis=-1, keepdims=True)
        tops.append(mx)
        idx = jnp.where(slot == k, ix, idx)
        x = jnp.where(lane == ix, -jnp.inf, x)
    e = [jnp.exp(v - tops[0]) for v in tops]
    inv = 1.0 / sum(e)
    gates = jnp.zeros((tm, LANES), F32)
    for k in range(TOP_K):
        gates = jnp.where(slot == k, e[k] * inv, gates)
    return idx.astype(jnp.int32), gates


def _bwd_chunk(c, n_ctx_chunks, n_chunks):
    return jnp.where(c < n_ctx_chunks, n_ctx_chunks - 1 - c, n_chunks - 1 - (c - n_ctx_chunks))


def _scan_streams(fwd_ref, bwd_ref):
    return [(bi, d, ref) for bi in range(SCAN_BATCH) for d, ref in ((0, fwd_ref), (1, bwd_ref))]


def _dot(a, b):
    return jnp.dot(a.astype(BF16), b.astype(BF16), preferred_element_type=F32)


def _dot_nt(a, b):
    return lax.dot_general(a.astype(BF16), b.astype(BF16), (((1,), (1,)), ((), ())),
                           preferred_element_type=F32)


def _dot_tn(a, b):
    return lax.dot_general(a.astype(BF16), b.astype(BF16), (((0,), (0,)), ((), ())),
                           preferred_element_type=F32)


def _dot_f32(a, b):
    return jnp.dot(a, b, preferred_element_type=F32, precision=HIGHEST)


def _seg_sum(x, ones_bd):
    hi = x.astype(BF16)
    lo = (x - hi.astype(F32)).astype(BF16)
    return (jnp.dot(hi, ones_bd, preferred_element_type=F32)
            + jnp.dot(lo, ones_bd, preferred_element_type=F32))


def _block_diag_ones(width, block):
    i = jnp.arange(width) // block
    return (i[:, None] == i[None, :]).astype(BF16)


def _hgrn_kernel(zf_ref, zb_ref, lo_ref, bd_ref, of_ref, ob_ref, st_ref):
    c = pl.program_id(1)
    n, sub = SCAN_CHUNK, HG_SUB
    nb = n // sub

    @pl.when(c == 0)
    def _():
        st_ref[...] = jnp.zeros_like(st_ref)

    row = lax.broadcasted_iota(jnp.int32, (n, n), 0)
    col = lax.broadcasted_iota(jnp.int32, (n, n), 1)
    colb = lax.broadcasted_iota(jnp.int32, (sub, LANES), 1)
    rown = lax.broadcasted_iota(jnp.int32, (sub, n), 0)
    coln = lax.broadcasted_iota(jnp.int32, (sub, n), 1)
    bd2 = bd_ref[...]

    dirs = []
    for bi, d, z_ref in _scan_streams(zf_ref, zb_ref):
        z = z_ref[bi]
        lower = lo_ref[d]
        qraw = z[:, :HG_DIM]
        sig = jax.nn.sigmoid(z[:, HG_DIM * (1 + d):HG_DIM * (2 + d)])
        v = z[:, 3 * HG_DIM:3 * HG_DIM + HG_VDIM]
        f = lower + (1.0 - lower) * sig
        g = jnp.log(jnp.maximum(f, HG_F_FLOOR))
        kin = (1.0 - lower) * (1.0 - sig)
        q = qraw * jax.nn.sigmoid(qraw)
        lag = (row - col) if d == 0 else (col - row)
        cum = _dot_f32((lag >= 0).astype(F32), g)
        cum_ex = cum - g
        tot = jnp.sum(g, axis=0, keepdims=True)
        blocks = []
        for j in range(nb):
            lo, hi = j * sub, (j + 1) * sub
            q_j, cum_j = q[lo:hi], cum[lo:hi]
            ref_j = cum_ex[lo:lo + 1] if d == 0 else cum_ex[hi - 1:hi]
            qt = q_j * jnp.exp(cum_j - ref_j)
            kt = kin * jnp.exp(jnp.minimum(ref_j - cum, 0.0))
            half_rows = sub // 2
            dead = jnp.zeros((half_rows, HG_DIM), F32)
            tiles = []
            for s in range(lo, hi):
                first_half_dead = d == 0 and s - lo >= half_rows
                second_half_dead = d == 1 and s - lo < half_rows
                t0 = lo + half_rows if first_half_dead else lo
                t1 = lo + half_rows if second_half_dead else hi
                live = q[t0:t1] * kin[s:s + 1] * jnp.exp(jnp.minimum(cum[t0:t1] - cum[s:s + 1], 0.0))
                tiles += [dead, live] if first_half_dead else [live, dead] if second_half_dead else [live]
            pair = jnp.concatenate(tiles, axis=0)
            blocks.append((qt, kt, pair))
        dirs.append(dict(bi=bi, d=d, v=v, qe=q * jnp.exp(cum), kdec=kin * jnp.exp(tot - cum),
                         etot=jnp.exp(tot), blocks=blocks))

    for dd in dirs:
        d = dd['d']
        for j, (qt, kt, pair) in enumerate(dd['blocks']):
            lo, hi = j * sub, (j + 1) * sub
            sums = [_dot(pair[:, 2 * LANES * hp:2 * LANES * (hp + 1)], bd2) for hp in range(HG_HEADS // 2)]
            for h in range(HG_HEADS):
                hs = slice(LANES * h, LANES * (h + 1))
                rs = sums[h // 2][:, LANES * (h % 2):LANES * (h % 2 + 1)]
                diag = jnp.zeros((sub, LANES), F32)
                for s in range(sub):
                    diag = jnp.where(colb == lo + s, rs[s * sub:(s + 1) * sub], diag)
                off = _dot_nt(qt[:, hs], kt[:, hs])
                if d == 0:
                    earlier, ordered = coln < lo, rown + lo >= coln
                else:
                    earlier, ordered = coln >= hi, rown + lo <= coln
                sc = jnp.where(earlier, off, jnp.where(ordered, diag[:, :n], 0.0))
                dd.setdefault(('out', h), []).append(_dot(sc, dd['v'][:, hs]))

    for dd in dirs:
        bi, d = dd['bi'], dd['d']
        o_ref = of_ref if d == 0 else ob_ref
        for h in range(HG_HEADS):
            hs = slice(LANES * h, LANES * (h + 1))
            slot = (bi * 2 + d) * HG_HEADS + h
            st = st_ref[slot]
            out = jnp.concatenate(dd[('out', h)], axis=0) + _dot_nt(dd['qe'][:, hs], st)
            o_ref[bi, :, hs] = out
            st_ref[slot] = (st * dd['etot'][:, hs]
                                        + _dot_tn(dd['v'][:, hs], dd['kdec'][:, hs]))


def hgrn2_scan(zb, lower, n_ctx):
    b, t, _ = zb.shape
    n, sb = SCAN_CHUNK, SCAN_BATCH
    assert b % sb == 0
    nc, ncc = t // n, n_ctx // n
    bwd = functools.partial(_bwd_chunk, n_ctx_chunks=ncc, n_chunks=nc)
    const = lambda shape: pl.BlockSpec(shape, lambda bi, c: (0,) * len(shape))
    return pl.pallas_call(
        _hgrn_kernel,
        grid=(b // sb, nc),
        in_specs=[pl.BlockSpec((sb, n, HG_IN), lambda bi, c: (bi, c, 0)),
                  pl.BlockSpec((sb, n, HG_IN), lambda bi, c: (bi, bwd(c), 0)),
                  const((2, 1, HG_DIM)), const((2 * LANES, 2 * LANES))],
        out_specs=[pl.BlockSpec((sb, n, HG_VDIM), lambda bi, c: (bi, c, 0)),
                   pl.BlockSpec((sb, n, HG_VDIM), lambda bi, c: (bi, bwd(c), 0))],
        out_shape=[jax.ShapeDtypeStruct((b, t, HG_VDIM), F32)] * 2,
        scratch_shapes=[pltpu.VMEM((sb * 2 * HG_HEADS, HG_KEY_DIM, HG_KEY_DIM), F32)],
        compiler_params=_cparams(("parallel", "arbitrary")),
    )(zb, zb, lower.reshape(2, 1, HG_DIM), _block_diag_ones(2 * LANES, LANES))


def _rwkv_iclr(z, d, a0_ref, a2_ref):
    ad = z[:, RW_LORA_OFF + LANES:RW_LORA_OFF + 2 * LANES]
    return jax.nn.sigmoid(a0_ref[d] + _dot(ad, a2_ref[d]))


def _rwkv_scan_kernel(zf_ref, zb_ref, w0_ref, w2_ref, a0_ref, a2_ref, kk_ref, ka_ref, bd_ref,
                      yf_ref, yb_ref, st_ref):
    c = pl.program_id(1)
    n, cdim = SCAN_CHUNK, RW_DIM
    npair = cdim // LANES

    @pl.when(c == 0)
    def _():
        st_ref[...] = jnp.zeros_like(st_ref)

    row = lax.broadcasted_iota(jnp.int32, (n, n), 0)
    col = lax.broadcasted_iota(jnp.int32, (n, n), 1)
    row4 = lax.broadcasted_iota(jnp.int32, (2 * n, 4 * n), 0) % n
    col4 = lax.broadcasted_iota(jnp.int32, (2 * n, 4 * n), 1) % n
    lane = lax.broadcasted_iota(jnp.int32, (1, LANES), 1)
    half = [lane < RW_HEAD_DIM, lane >= RW_HEAD_DIM]

    def stack2(x):
        return jnp.concatenate([jnp.where(half[0], x, 0.0), jnp.where(half[1], x, 0.0)], axis=0)

    r2 = lax.broadcasted_iota(jnp.int32, (LANES, LANES), 0) < RW_HEAD_DIM
    c2 = lax.broadcasted_iota(jnp.int32, (LANES, LANES), 1) < RW_HEAD_DIM
    same_head = r2 == c2

    chains = []
    for bi, d, z_ref in _scan_streams(zf_ref, zb_ref):
        z = z_ref[bi]
        r, k, v = z[:, :cdim], z[:, cdim:2 * cdim], z[:, 2 * cdim:3 * cdim]
        wd = z[:, RW_LORA_OFF:RW_LORA_OFF + LANES]
        w_pre = w0_ref[d] + _dot(jnp.tanh(wd), w2_ref[d])
        lw = -jnp.exp(-0.5) * jax.nn.sigmoid(w_pre)
        a = _rwkv_iclr(z, d, a0_ref, a2_ref)
        kk = k * kk_ref[...]
        kk = kk * lax.rsqrt(jnp.maximum(_seg_sum(kk * kk, bd_ref[...]), 1e-24))
        kd = k * (1.0 + (a - 1.0) * ka_ref[...])
        bb = kk * a
        lag = (row - col) if d == 0 else (col - row)
        lag4 = (row4 - col4) if d == 0 else (col4 - row4)
        incl4, strict4 = lag4 >= 0, lag4 > 0
        cum = _dot_f32((lag >= 0).astype(F32), lw)
        tot = jnp.sum(lw, axis=0, keepdims=True)
        g_inv = jnp.exp(-cum)
        g_tail = jnp.exp(tot - cum)
        rt = r * jnp.exp(cum)
        at = -kk * jnp.exp(cum - lw)
        etot = jnp.exp(tot)
        bt, kt = bb * g_inv, kd * g_inv
        tail = jnp.concatenate([bb * g_tail, kd * g_tail], axis=0)
        for p in range(npair):
            ps = slice(LANES * p, LANES * (p + 1))
            chains.append(dict(b=bi, d=d, p=p, ps=ps, incl4=incl4, strict4=strict4, tail=tail[:, ps],
                               at=stack2(at[:, ps]), rt=rt[:, ps], bt=bt[:, ps], kt=kt[:, ps],
                               v=v[:, ps], etot=etot[:, ps]))

    for ch in chains:
        left = jnp.concatenate([ch['at'], stack2(ch['rt'])], axis=0)
        right = jnp.concatenate([stack2(ch['bt']), stack2(ch['kt'])], axis=0)
        prod = _dot_nt(left, right)
        ch['top'] = jnp.where(ch['strict4'], prod[:2 * n], 0.0)
        ch['bot'] = jnp.where(ch['incl4'], prod[2 * n:], 0.0)
    for ch in chains:
        ch['vs'] = stack2(ch['v'])
        ch['x'] = jnp.concatenate([ch['at'], _dot(ch['top'][:, 2 * n:], ch['vs'])], axis=1)
        ch['a'] = ch['top'][:, :2 * n]
    steps = n.bit_length() - 1
    for i in range(steps):
        for ch in chains:
            ch['x'] = ch['x'] + _dot(ch['a'], ch['x'])
            if i + 1 < steps:
                ch['a'] = _dot(ch['a'], ch['a'])
    for ch in chains:
        x = ch['x'][:n] + ch['x'][n:]
        slot = (ch['b'] * 2 + ch['d']) * npair + ch['p']
        st = st_ref[slot]
        ws = _dot_nt(jnp.concatenate([x[:, :LANES], ch['rt']], axis=0), st)
        u = ws[:n] + x[:, LANES:]
        ys = _dot(ch['bot'], jnp.concatenate([stack2(u), ch['vs']], axis=0))
        y_ref = yf_ref if ch['d'] == 0 else yb_ref
        y_ref[ch['b'], :, ch['ps']] = ws[n:] + ys[:n] + ys[n:]
        delta = _dot_tn(jnp.concatenate([u, ch['v']], axis=0), ch['tail'])
        st_ref[slot] = st * ch['etot'] + jnp.where(same_head, delta, 0.0)


def _pad_lora(w, d):
    return jnp.concatenate([w[d] if i == d else jnp.zeros_like(w[i]) for i in range(2)], axis=0)


def rwkv7_scan(za, w0, w2, a0, a2, k_k, k_a, n_ctx):
    b, t, _ = za.shape
    n, cdim, sb = SCAN_CHUNK, RW_DIM, SCAN_BATCH
    assert b % sb == 0
    nc, ncc = t // n, n_ctx // n
    bwd = functools.partial(_bwd_chunk, n_ctx_chunks=ncc, n_chunks=nc)
    const = lambda shape: pl.BlockSpec(shape, lambda bi, c: (0,) * len(shape))
    w2p = jnp.stack([_pad_lora(w2, d) for d in range(2)]).astype(BF16)
    a2p = jnp.stack([_pad_lora(a2, d) for d in range(2)]).astype(BF16)
    return pl.pallas_call(
        _rwkv_scan_kernel,
        grid=(b // sb, nc),
        in_specs=[pl.BlockSpec((sb, n, RW_IN), lambda bi, c: (bi, c, 0)),
                  pl.BlockSpec((sb, n, RW_IN), lambda bi, c: (bi, bwd(c), 0)),
                  const((2, 1, cdim)), const((2, LANES, cdim)), const((2, 1, cdim)),
                  const((2, LANES, cdim)), const((1, cdim)), const((1, cdim)), const((cdim, cdim))],
        out_specs=[pl.BlockSpec((sb, n, cdim), lambda bi, c: (bi, c, 0)),
                   pl.BlockSpec((sb, n, cdim), lambda bi, c: (bi, bwd(c), 0))],
        out_shape=[jax.ShapeDtypeStruct((b, t, cdim), F32)] * 2,
        scratch_shapes=[pltpu.VMEM((sb * 2 * cdim // LANES, LANES, LANES), F32)],
        compiler_params=_cparams(("parallel", "arbitrary")),
    )(za, za, w0.reshape(2, 1, cdim), w2p, a0.reshape(2, 1, cdim), a2p, k_k.reshape(1, cdim),
      k_a.reshape(1, cdim), _block_diag_ones(cdim, RW_HEAD_DIM))


def _even_post_kernel(za_ref, yf_ref, yb_ref, zg_ref, of_ref, ob_ref, a0_ref, a2_ref, g2_ref,
                      ka_ref, rk_ref, gng_ref, gnb_ref, hgg_ref, bd64_ref, bd128_ref, wout_ref,
                      o_ref):
    cdim = RW_DIM
    z = za_ref[...]
    r, k, v = z[:, :cdim], z[:, cdim:2 * cdim], z[:, 2 * cdim:3 * cdim]
    a_sum = _rwkv_iclr(z, 0, a0_ref, a2_ref) + _rwkv_iclr(z, 1, a0_ref, a2_ref)
    k_sum = k * (2.0 + (a_sum - 2.0) * ka_ref[...])
    gate = _dot(jax.nn.sigmoid(z[:, RW_LORA_OFF + 2 * LANES:]), g2_ref[...])
    bd64 = bd64_ref[...]
    y = yf_ref[...] + yb_ref[...]
    mean = _seg_sum(y, bd64) * (1.0 / RW_HEAD_DIM)
    cen = y - mean
    var = _seg_sum(cen * cen, bd64) * (1.0 / RW_HEAD_DIM)
    y = cen * lax.rsqrt(var + RW_GN_EPS) * gng_ref[...] + gnb_ref[...]
    bonus = _seg_sum(r * k_sum * rk_ref[...], bd64) * v
    y_rw = (y + bonus) * gate

    o = of_ref[...] + ob_ref[...]
    ms = _seg_sum(o * o, bd128_ref[...]) * (1.0 / HG_KEY_DIM)
    g_hg = zg_ref[...]
    y_hg = o * lax.rsqrt(ms + RMS_EPS) * hgg_ref[...] * (g_hg * jax.nn.sigmoid(g_hg))

    y_all = jnp.concatenate([y_rw, y_hg], axis=1).astype(BF16)
    o_ref[...] = jnp.dot(y_all, wout_ref[...], preferred_element_type=F32)


def even_post(za, yf, yb, zb, of, ob, a0, a2, g2, k_a, r_k, gn_g, gn_b, hg_norm_g, w_out, tm=256):
    m = za.shape[0]
    cdim, dm = RW_DIM, w_out.shape[1]
    tm = _row_tile(m, tm)
    rows = lambda width, blk=0: pl.BlockSpec((tm, width), lambda i: (i, blk))
    const = lambda shape: pl.BlockSpec(shape, lambda i: (0,) * len(shape))
    a2p = jnp.stack([_pad_lora(a2, d) for d in range(2)]).astype(BF16)
    vec = lambda x: x.reshape(1, cdim)
    return pl.pallas_call(
        _even_post_kernel,
        grid=(m // tm,),
        in_specs=[rows(RW_IN), rows(cdim), rows(cdim), rows(HG_VDIM, (HG_IN - HG_VDIM) // HG_VDIM),
                  rows(HG_VDIM), rows(HG_VDIM),
                  const((2, 1, cdim)), const((2, LANES, cdim)), const((RW_GATE_LORA, cdim)),
                  const((1, cdim)), const((1, cdim)), const((1, cdim)), const((1, cdim)),
                  const((1, HG_VDIM)), const((cdim, cdim)), const((HG_VDIM, HG_VDIM)),
                  const((cdim + HG_VDIM, dm))],
        out_specs=rows(dm),
        out_shape=jax.ShapeDtypeStruct((m, dm), F32),
        compiler_params=_cparams(("parallel",)),
    )(za, yf, yb, zb, of, ob, a0.reshape(2, 1, cdim), a2p, g2.astype(BF16), vec(k_a), vec(r_k),
      vec(gn_g), vec(gn_b), hg_norm_g.reshape(1, HG_VDIM), _block_diag_ones(cdim, RW_HEAD_DIM),
      _block_diag_ones(HG_VDIM, HG_KEY_DIM), w_out.astype(BF16))


MLA_SLOT = 2 * LANES
MLA_GROUP = 4


def _rot_half(w):
    r1, r2, c1, c2 = jnp.split(w, 4, axis=-1)
    return jnp.concatenate([-r2, r1, -c2, c1], axis=-1)


def _mla_kv_kernel(z_ref, cs_ref, g_ref, w_ref, o_ref):
    z = z_ref[...]
    kva = z[:, :MLA_KV_RANK]
    kva = kva * lax.rsqrt(jnp.mean(kva * kva, axis=-1, keepdims=True) + RMS_EPS) * g_ref[...]
    cs = cs_ref[...]
    kpe = (z[:, MLA_KV_RANK:MLA_KV_RANK + MLA_ROPE] * cs[:, :MLA_ROPE]
           + z[:, MLA_KV_RANK + MLA_ROPE:] * cs[:, MLA_ROPE:])
    x = jnp.concatenate([kva, kpe], axis=1).astype(BF16)
    o_ref[...] = jnp.dot(x, w_ref[...], preferred_element_type=F32).astype(o_ref.dtype)


def mla_kv(z_kv, cs, kv_norm_g, w_kv_slots, rows_per_seq, tm=256):
    m, width = z_kv.shape
    tm = _row_tile(rows_per_seq, tm)
    per_seq = rows_per_seq // tm
    n_out = w_kv_slots.shape[1]
    return pl.pallas_call(
        _mla_kv_kernel,
        grid=(m // tm,),
        in_specs=[pl.BlockSpec((tm, width), lambda i: (i, 0)),
                  pl.BlockSpec((tm, 2 * MLA_ROPE), lambda i: (i % per_seq, 0)),
                  pl.BlockSpec((1, MLA_KV_RANK), lambda i: (0, 0)),
                  pl.BlockSpec(w_kv_slots.shape, lambda i: (0, 0))],
        out_specs=pl.BlockSpec((tm, n_out), lambda i: (i, 0)),
        out_shape=jax.ShapeDtypeStruct((m, n_out), BF16),
        compiler_params=_cparams(("parallel",)),
    )(z_kv, cs, kv_norm_g.reshape(1, MLA_KV_RANK), w_kv_slots)


def _mla_attn_kernel(zq_ref, kv_ref, cos_ref, sin_ref, g_ref, wq_ref, wr_ref, o_ref):
    zq = zq_ref[0]
    zn = (zq * lax.rsqrt(jnp.mean(zq * zq, axis=-1, keepdims=True) + RMS_EPS) * g_ref[...]).astype(BF16)
    a = jnp.dot(zn, wq_ref[...], preferred_element_type=F32)
    ar = jnp.dot(zn, wr_ref[...], preferred_element_type=F32)
    cos, sin = cos_ref[...], sin_ref[...]
    low = lax.broadcasted_iota(jnp.int32, (1, LANES), 1) < MLA_V
    blocks = []
    for pair in range(MLA_GROUP // 2):
        res = []
        for j in range(2):
            h = 2 * pair + j
            q_nope = a[:, MLA_SLOT * h:MLA_SLOT * h + LANES]
            q_rope = (a[:, MLA_SLOT * h + LANES:MLA_SLOT * (h + 1)] * cos
                      + ar[:, LANES * h:LANES * (h + 1)] * sin)
            qf = (jnp.concatenate([q_nope, q_rope], axis=1) * MLA_SCALE).astype(BF16)
            s = lax.dot_general(qf, kv_ref[0, :, MLA_SLOT * h:MLA_SLOT * (h + 1)],
                                (((1,), (1,)), ((), ())), preferred_element_type=F32)
            p = jnp.exp(s - jnp.max(s, axis=-1, keepdims=True))
            l = jnp.sum(p, axis=-1, keepdims=True)
            res.append(jnp.dot(p.astype(BF16), kv_ref[0, :, MLA_SLOT * h:MLA_SLOT * h + LANES],
                               preferred_element_type=F32) / l)
        blocks.append(jnp.where(low, res[1], res[0]))
    o_ref[0] = jnp.concatenate(blocks, axis=1).astype(o_ref.dtype)


def mla_attention(zq, kv_slots, n_keys, cos, sin, q_norm_g, wq_slots, wr_slots, tq=1024):
    b, nq, _ = zq.shape
    tq = _row_tile(nq, tq)
    gw = MLA_GROUP * MLA_SLOT
    return pl.pallas_call(
        _mla_attn_kernel,
        grid=(b, MLA_HEADS // MLA_GROUP, nq // tq),
        in_specs=[pl.BlockSpec((1, tq, MLA_Q_RANK), lambda bi, g, qi: (bi, qi, 0)),
                  pl.BlockSpec((1, n_keys, gw), lambda bi, g, qi: (bi, 0, g)),
                  pl.BlockSpec((tq, LANES), lambda bi, g, qi: (qi, 0)),
                  pl.BlockSpec((tq, LANES), lambda bi, g, qi: (qi, 0)),
                  pl.BlockSpec((1, MLA_Q_RANK), lambda bi, g, qi: (0, 0)),
                  pl.BlockSpec((MLA_Q_RANK, gw), lambda bi, g, qi: (0, g)),
                  pl.BlockSpec((MLA_Q_RANK, gw // 2), lambda bi, g, qi: (0, g))],
        out_specs=pl.BlockSpec((1, tq, MLA_GROUP * MLA_V), lambda bi, g, qi: (bi, qi, g)),
        out_shape=jax.ShapeDtypeStruct((b, nq, MLA_HEADS * MLA_V), BF16),
        compiler_params=_cparams(("parallel", "parallel", "parallel")),
    )(zq, kv_slots, cos, sin, q_norm_g.reshape(1, MLA_Q_RANK), wq_slots, wr_slots)


def _moe_kernel(te_ref, nt_ref, x_ref, wgu_ref, bgu_ref, wd_ref, bd_ref, o_ref, wgu_bf, wd_bf):
    i = pl.program_id(0)
    live = i < nt_ref[0]
    new_expert = jnp.logical_or(i == 0, te_ref[i] != te_ref[jnp.maximum(i - 1, 0)])

    @pl.when(jnp.logical_and(live, new_expert))
    def _():
        wgu_bf[...] = wgu_ref[0, 0].astype(BF16)
        wd_bf[...] = wd_ref[0, 0].astype(BF16)

    @pl.when(live)
    def _():
        h = jnp.dot(x_ref[...], wgu_bf[...], preferred_element_type=F32) + bgu_ref[0, 0]
        glu = jnp.minimum(h[:, :EXPERT_DIM], SWIGLU_LIMIT)
        lin = jnp.clip(h[:, EXPERT_DIM:], -SWIGLU_LIMIT, SWIGLU_LIMIT)
        act = glu * jax.nn.sigmoid(SWIGLU_ALPHA * glu) * (lin + 1.0)
        y = jnp.dot(act.astype(BF16), wd_bf[...], preferred_element_type=F32) + bd_ref[0, 0]
        o_ref[...] = y.astype(o_ref.dtype)

    @pl.when(i >= nt_ref[0])
    def _():
        o_ref[...] = jnp.zeros_like(o_ref)


def moe_grouped(x_sorted, tile_expert, n_tiles_used, layer, w_gu, b_gu, w_down, b_down):
    p, dm = x_sorted.shape
    tm = MOE_TILE
    nl, e, _, f2 = w_gu.shape
    grid_spec = pltpu.PrefetchScalarGridSpec(
        num_scalar_prefetch=2,
        grid=(p // tm,),
        in_specs=[
            pl.BlockSpec((tm, dm), lambda i, te, nt: (i, 0)),
            pl.BlockSpec((1, 1, dm, f2), lambda i, te, nt: (layer, te[i], 0, 0)),
            pl.BlockSpec((1, 1, 1, f2), lambda i, te, nt: (layer, te[i], 0, 0)),
            pl.BlockSpec((1, 1, f2 // 2, dm), lambda i, te, nt: (layer, te[i], 0, 0)),
            pl.BlockSpec((1, 1, 1, dm), lambda i, te, nt: (layer, te[i], 0, 0)),
        ],
        out_specs=pl.BlockSpec((tm, dm), lambda i, te, nt: (i, 0)),
        scratch_shapes=[pltpu.VMEM((dm, f2), BF16), pltpu.VMEM((f2 // 2, dm), BF16)],
    )
    return pl.pallas_call(
        _moe_kernel,
        grid_spec=grid_spec,
        out_shape=jax.ShapeDtypeStruct((p, dm), F32),
        compiler_params=_cparams(("arbitrary",)),
    )(tile_expert, n_tiles_used, x_sorted, w_gu, b_gu.reshape(nl, e, 1, f2), w_down,
      b_down.reshape(nl, e, 1, dm))


def moe(idx_lanes, u16, layer, w_gu, b_gu, w_down, b_down, row0, n):
    dm = u16.shape[1]
    tm = MOE_TILE
    top_idx = idx_lanes[row0:row0 + n, :TOP_K]
    e_flat = top_idx.T.reshape(-1).astype(jnp.int32)
    npair = n * TOP_K
    rb = _row_tile(npair, RANK_BLOCK)
    onehot = (e_flat[:, None] == jnp.arange(N_EXPERTS, dtype=jnp.int32)[None, :])
    oh3 = onehot.astype(BF16).reshape(npair // rb, rb, N_EXPERTS)
    earlier = (jnp.arange(rb)[:, None] > jnp.arange(rb)[None, :]).astype(BF16)
    within = jnp.einsum('ij,bje->bie', earlier, oh3, preferred_element_type=F32)
    blk_tot = jnp.sum(oh3.astype(F32), axis=1)
    blk_off = jnp.cumsum(blk_tot, axis=0) - blk_tot
    counts = jnp.sum(blk_tot, axis=0).astype(jnp.int32)
    padded = ((counts + tm - 1) // tm) * tm
    ends_p = jnp.cumsum(padded)
    starts_p = ends_p - padded
    starts = jnp.cumsum(counts) - counts
    rank = within + blk_off[:, None, :] + starts_p.astype(F32)[None, None, :]
    pos = jnp.sum(oh3.astype(F32) * rank, axis=-1).astype(jnp.int32).reshape(npair)
    p_rows = npair + N_EXPERTS * tm
    n_tiles = p_rows // tm
    tile_start = jnp.arange(n_tiles, dtype=jnp.int32) * tm
    tile_expert = jnp.minimum(jnp.sum((ends_p[None, :] <= tile_start[:, None]).astype(jnp.int32), axis=1),
                              N_EXPERTS - 1)
    n_used = (ends_p[-1] // tm).astype(jnp.int32).reshape(1)
    _, sorted_tok = lax.sort_key_val(e_flat, jnp.arange(npair, dtype=jnp.int32) % n)
    onehot_t = (tile_expert[:, None] == jnp.arange(N_EXPERTS, dtype=jnp.int32)[None, :]).astype(jnp.int32)
    shift = jnp.sum(onehot_t * (starts - starts_p)[None, :], axis=1)
    last = jnp.sum(onehot_t * (starts + counts)[None, :], axis=1)
    rows = jnp.arange(p_rows, dtype=jnp.int32).reshape(n_tiles, tm)
    src = rows + shift[:, None]
    valid = src < last[:, None]
    src_tok = jnp.where(valid, sorted_tok[jnp.where(valid, src, rows % npair).reshape(-1)].reshape(n_tiles, tm),
                        rows % n).reshape(-1)
    x_sorted = u16[src_tok + row0]
    y_sorted = moe_grouped(x_sorted, tile_expert, n_used, layer, w_gu, b_gu, w_down, b_down)
    return y_sorted[pos].reshape(TOP_K, n, dm)


def _norm_mod_kernel(*refs, dm, gate, mod, has_norm, combine, tiles_per_part, route, outs):
    refs = list(refs)
    h = refs.pop(0)[...]
    if has_norm:
        if combine:
            parts = [refs.pop(0) for _ in range(combine)]
            slot_gate = refs.pop(0)[...]
            part = pl.program_id(0) // tiles_per_part
            y = None
            for p, y_ref in enumerate(parts):
                yp = sum(y_ref[k] * slot_gate[:, k:k + 1] for k in range(TOP_K))
                y = yp if y is None else jnp.where(part == p, yp, y)
        else:
            y = refs.pop(0)[...]
        mg = refs.pop(0)
        lng, lnb = refs.pop(0)[...], refs.pop(0)[...]
        x = DEEPNORM_ALPHA * h + mg[0, :, gate * dm:(gate + 1) * dm] * y
        mu = jnp.mean(x, axis=-1, keepdims=True)
        cen = x - mu
        var = jnp.mean(cen * cen, axis=-1, keepdims=True)
        h = cen * lax.rsqrt(var + LN_EPS) * lng + lnb
    if mod is not None:
        mm = refs.pop(0)
        shift, scale = mod
        u = h * (1.0 + mm[0, :, scale * dm:(scale + 1) * dm]) + mm[0, :, shift * dm:(shift + 1) * dm]
    vals = {'h': h}
    if mod is not None:
        vals['u16'] = u
    if route:
        wr, br = refs.pop(0)[...], refs.pop(0)[...]
        vals['idx'], vals['gates'] = _route_top4(u, wr, br)
    for kind, o_ref in zip(outs, refs):
        o_ref[...] = vals[kind].astype(o_ref.dtype)


def norm_mod(h, y, mods, rows_per_seq, n_ctx, n_batch, gate=None, ln=None, mod=None, outs=('h',),
             slot_gates=None, router=None, tm=256):
    m, dm = h.shape
    tm = _row_tile(n_ctx, tm)
    per_seq, ctx_tiles = rows_per_seq // tm, n_ctx // tm
    rows = mods.shape[0] // DEPTH

    def mod_row(layer):
        return lambda i: (layer * rows + jnp.where(i % per_seq < ctx_tiles, n_batch, i // per_seq), 0, 0)

    tile = pl.BlockSpec((tm, dm), lambda i: (i, 0))
    vec = pl.BlockSpec((1, dm), lambda i: (0, 0))
    args, specs = [h], [tile]
    tiles_per_part = 0
    if gate is not None:
        if slot_gates is None:
            args.append(y)
            specs.append(tile)
        else:
            tiles_per_part = m // tm // len(y)
            for p, part in enumerate(y):
                args.append(part)
                specs.append(pl.BlockSpec(
                    (TOP_K, tm, dm),
                    lambda i, p=p: (0, jnp.clip(i - p * tiles_per_part, 0, tiles_per_part - 1), 0)))
            args.append(slot_gates)
            specs.append(pl.BlockSpec((tm, LANES), lambda i: (i, 0)))
        args += [mods, ln[0].reshape(1, dm), ln[1].reshape(1, dm)]
        specs += [pl.BlockSpec((1, 1, N_MOD * dm), mod_row(gate[0])), vec, vec]
    if mod is not None:
        args.append(mods)
        specs.append(pl.BlockSpec((1, 1, N_MOD * dm), mod_row(mod[0])))
    if router is not None:
        w_router, b_router = router
        args += [w_router, b_router.reshape(1, N_EXPERTS)]
        specs += [pl.BlockSpec((dm, N_EXPERTS), lambda i: (0, 0)), pl.BlockSpec((1, N_EXPERTS), lambda i: (0, 0))]
    dtypes = {'h': (dm, F32), 'u16': (dm, BF16), 'idx': (LANES, jnp.int32), 'gates': (LANES, F32)}
    res = pl.pallas_call(
        functools.partial(_norm_mod_kernel, dm=dm, gate=None if gate is None else gate[1],
                          mod=None if mod is None else mod[1:], has_norm=gate is not None,
                          combine=0 if slot_gates is None else len(y), tiles_per_part=tiles_per_part,
                          route=router is not None, outs=outs),
        grid=(m // tm,),
        in_specs=specs,
        out_specs=[pl.BlockSpec((tm, dtypes[k][0]), lambda i: (i, 0)) for k in outs],
        out_shape=[jax.ShapeDtypeStruct((m,) + dtypes[k][:1], dtypes[k][1]) for k in outs],
        compiler_params=_cparams(("parallel",)),
    )(*args)
    return res[0] if len(outs) == 1 else res


SUBLANES = 8


def _shift_kernel(z_ref, prev_ref, next_ref, mu_ref, o_ref, *, per_seq, ctx_tiles):
    j = pl.program_id(0) % per_seq
    z = z_ref[...]
    tm = z.shape[0]
    starts_part = jnp.logical_or(j == 0, j == ctx_tiles)
    ends_part = jnp.logical_or(j == ctx_tiles - 1, j == per_seq - 1)
    before = prev_ref[SUBLANES - 1:SUBLANES, :] * jnp.where(starts_part, 0.0, 1.0)
    after = next_ref[0:1, :] * jnp.where(ends_part, 0.0, 1.0)
    row = lax.broadcasted_iota(jnp.int32, (tm, 1), 0)
    prev = jnp.where(row == 0, before, pltpu.roll(z, 1, axis=0))
    nxt = jnp.where(row == tm - 1, after, pltpu.roll(z, tm - 1, axis=0))
    o_ref[...] = z + mu_ref[...] * (0.5 * (prev + nxt) - z)


def token_shift(z, mu, rows_per_seq, n_ctx, tm=256):
    m, width = z.shape
    tm = _row_tile(n_ctx, tm)
    per_tile = tm // SUBLANES
    last = m // SUBLANES - 1
    return pl.pallas_call(
        functools.partial(_shift_kernel, per_seq=rows_per_seq // tm, ctx_tiles=n_ctx // tm),
        grid=(m // tm,),
        in_specs=[pl.BlockSpec((tm, width), lambda i: (i, 0)),
                  pl.BlockSpec((SUBLANES, width), lambda i: (jnp.maximum(i * per_tile - 1, 0), 0)),
                  pl.BlockSpec((SUBLANES, width), lambda i: (jnp.minimum((i + 1) * per_tile, last), 0)),
                  pl.BlockSpec((1, width), lambda i: (0, 0))],
        out_specs=pl.BlockSpec((tm, width), lambda i: (i, 0)),
        out_shape=jax.ShapeDtypeStruct((m, width), F32),
        compiler_params=_cparams(("parallel",)),
    )(z, z, z, mu.reshape(1, width))


def _even_mixer(u, b, t, n_ctx, w_in, mu, w0, w2, a0, a2, g2, k_k, k_a, r_k, gn_g, gn_b, lower,
                hg_norm_g, w_out):
    za, zb = matmul(u, w_in, splits=(RW_IN, HG_IN))
    za = token_shift(za, mu, t, n_ctx).reshape(b, t, RW_IN)
    zb = zb.reshape(b, t, HG_IN)
    yf, yb = rwkv7_scan(za, w0, w2, a0, a2, k_k, k_a, n_ctx)
    of, ob = hgrn2_scan(zb, lower, n_ctx)
    flat = lambda x: x.reshape(b * t, x.shape[-1])
    y = even_post(flat(za), flat(yf), flat(yb), flat(zb), flat(of), flat(ob), a0, a2, g2, k_a,
                  r_k.reshape(-1), gn_g, gn_b, hg_norm_g, w_out)
    return y


def _rope_tables(rows):
    t = jnp.arange(rows * GRID_W)
    row = (t // GRID_W).astype(F32)
    col = (t % GRID_W).astype(F32)
    half = MLA_ROPE // 2
    inv_freq = ROPE_BASE ** (-jnp.arange(0, half, 2, dtype=F32) / half)
    ang_r = row[:, None] * inv_freq
    ang_c = col[:, None] * inv_freq
    ang = jnp.concatenate([ang_r, ang_r, ang_c, ang_c], axis=-1)
    return jnp.cos(ang), jnp.sin(ang)


def _mla_weights(w_in, w_qb, w_kvb, w_out):
    hh, half = MLA_HEADS, MLA_HEADS // 2
    w_in_ext = jnp.concatenate([w_in, _rot_half(w_in[:, MLA_Q_RANK + MLA_KV_RANK:])], axis=1)
    kvb = w_kvb.reshape(MLA_KV_RANK, half, 2, 2, MLA_NOPE)
    kv128 = jnp.stack([kvb[:, :, 0], kvb[:, :, 1, ::-1]], axis=2).reshape(MLA_KV_RANK, hh, LANES)
    kv_rows = jnp.pad(kv128, ((0, 0), (0, 0), (0, MLA_SLOT - LANES))).reshape(MLA_KV_RANK, hh * MLA_SLOT)
    rope_rows = jnp.pad(jnp.eye(MLA_ROPE, dtype=F32), ((0, 0), (LANES, MLA_SLOT - LANES - MLA_ROPE)))
    w_kv_slots = jnp.concatenate([kv_rows, jnp.tile(rope_rows, (1, hh))], axis=0).astype(BF16)
    qb = w_qb.reshape(MLA_Q_RANK, hh, MLA_QK)
    nope = qb[..., :MLA_NOPE].reshape(MLA_Q_RANK, half, 2, MLA_NOPE)
    zeros = jnp.zeros_like(nope[:, :, 0])
    nope128 = jnp.stack([jnp.concatenate([nope[:, :, 0], zeros], -1),
                         jnp.concatenate([zeros, nope[:, :, 1]], -1)], axis=2).reshape(MLA_Q_RANK, hh, LANES)
    rope = qb[..., MLA_NOPE:]
    pad_rope = lambda x: jnp.pad(x, ((0, 0), (0, 0), (0, LANES - MLA_ROPE)))
    wq_slots = jnp.concatenate([nope128, pad_rope(rope)], -1).reshape(MLA_Q_RANK, hh * MLA_SLOT)
    wr_slots = pad_rope(_rot_half(rope)).reshape(MLA_Q_RANK, hh * LANES)
    w_out_perm = w_out.reshape(half, 2, MLA_V, -1)[:, ::-1].reshape(hh * MLA_V, -1)
    return w_in_ext, w_kv_slots, wq_slots.astype(BF16), wr_slots.astype(BF16), w_out_perm


def _mla_mixer(u, b, t, n_ctx, w_in, q_norm_g, w_qb, kv_norm_g, w_kvb, w_out, cos, sin):
    w_in_ext, w_kv_slots, wq_slots, wr_slots, w_out_perm = _mla_weights(w_in, w_qb, w_kvb, w_out)
    zq, z_kv = matmul(u, w_in_ext, splits=(MLA_Q_RANK, MLA_KV_RANK + 2 * MLA_ROPE))
    ones, zeros = jnp.ones((n_ctx, MLA_ROPE), F32), jnp.zeros((n_ctx, MLA_ROPE), F32)
    cs = jnp.concatenate([jnp.concatenate([ones, cos], axis=0), jnp.concatenate([zeros, sin], axis=0)], axis=1)
    kv_slots = mla_kv(z_kv, cs, kv_norm_g, w_kv_slots, t).reshape(b, t, MLA_HEADS * MLA_SLOT)
    zq = zq.reshape(b, t, MLA_Q_RANK)
    widen = lambda x, fill: jnp.pad(x, ((0, 0), (0, LANES - MLA_ROPE)), constant_values=fill)
    o_ctx = mla_attention(zq[:, :n_ctx], kv_slots, n_ctx, widen(ones, 1.0), widen(zeros, 0.0), q_norm_g,
                          wq_slots, wr_slots)
    o_lat = mla_attention(zq[:, n_ctx:], kv_slots, t, widen(cos, 1.0), widen(sin, 0.0), q_norm_g,
                          wq_slots, wr_slots)
    o = jnp.concatenate([o_ctx, o_lat], axis=1).reshape(b * t, MLA_HEADS * MLA_V)
    return matmul(o, w_out_perm)


def kernel(x, c, ctx, c_ctx, mod_w, mod_b, ln_g, ln_b, ev_w_in, rw_mu, rw_w0, rw_w2, rw_a0, rw_a2,
           rw_g2, rw_k_k, rw_k_a, rw_r_k, rw_gn_g, rw_gn_b, hg_lb, hg_norm_g, ev_w_out, od_w_in,
           mla_q_norm_g, mla_w_qb, mla_kv_norm_g, mla_w_kvb, od_w_out, moe_w_router, moe_b_router,
           moe_w_gu, moe_b_gu, moe_w_down, moe_b_down):
    b, n_lat, dm = x.shape
    n_ctx = ctx.shape[1]
    t = n_ctx + n_lat
    cos, sin = _rope_tables(n_lat // GRID_W)
    lb = jax.nn.softmax(hg_lb.astype(F32), axis=0)
    hg_lower = jnp.cumsum(lb, axis=0) - lb[0]
    c_act = c * jax.nn.sigmoid(c)
    cc_act = c_ctx * jax.nn.sigmoid(c_ctx)
    mod_in = jnp.concatenate([c_act, cc_act[None]], axis=0)
    pad = (-mod_in.shape[0]) % 8
    mod_in = jnp.pad(mod_in, ((0, pad), (0, 0)))
    mods = jnp.stack([matmul(mod_in, mod_w[layer]) + mod_b[layer] for layer in range(DEPTH)])
    mods = mods.reshape(DEPTH * mod_in.shape[0], 1, N_MOD * dm)
    nm = functools.partial(norm_mod, mods=mods, rows_per_seq=t, n_ctx=n_ctx, n_batch=b)
    h = jnp.concatenate([ctx, x], axis=1).reshape(b * t, dm)
    assert b % MOE_PARTS == 0
    part_rows = b * t // MOE_PARTS
    u16 = nm(h, None, mod=(0, 0, 1), outs=('u16',))
    for layer in range(DEPTH):
        j = layer // 2
        if layer % 2 == 0:
            y = _even_mixer(u16, b, t, n_ctx, ev_w_in[j], rw_mu[j], rw_w0[j], rw_w2[j], rw_a0[j],
                            rw_a2[j], rw_g2[j], rw_k_k[j], rw_k_a[j], rw_r_k[j], rw_gn_g[j], rw_gn_b[j],
                            hg_lower[j], hg_norm_g[j], ev_w_out[j])
        else:
            y = _mla_mixer(u16, b, t, n_ctx, od_w_in[j], mla_q_norm_g[j], mla_w_qb[j],
                           mla_kv_norm_g[j], mla_w_kvb[j], od_w_out[j], cos, sin)
        h, u16, idx, gts = nm(h, y, gate=(layer, 2), ln=(ln_g[layer, 0], ln_b[layer, 0]),
                              mod=(layer, 3, 4), outs=('h', 'u16', 'idx', 'gates'),
                              router=(moe_w_router[layer], moe_b_router[layer]))
        f = [moe(idx, u16, layer, moe_w_gu, moe_b_gu, moe_w_down, moe_b_down, p * part_rows, part_rows)
             for p in range(MOE_PARTS)]
        fg = gts
        if layer + 1 < DEPTH:
            h, u16 = nm(h, f, gate=(layer, 5), ln=(ln_g[layer, 1], ln_b[layer, 1]),
                        mod=(layer + 1, 0, 1), outs=('h', 'u16'), slot_gates=fg)
        else:
            h = nm(h, f, gate=(layer, 5), ln=(ln_g[layer, 1], ln_b[layer, 1]), outs=('h',),
                   slot_gates=fg)
    return h.reshape(b, t, dm)[:, n_ctx:]
```
